```python
import jax, jax.numpy as jnp
from jax import lax
import numpy as np

D_MODEL = 2048
BATCH = 2
SEQ = 4096
DEPTH = 2
DEC_BATCH = 128
DEC_SEQ = 4
PAST_LEN = 8192
PAGE_SIZE = 128

N_LAYERS_A = DEPTH // 2
N_LAYERS_B = DEPTH - N_LAYERS_A
HD_A = 128
H_A = D_MODEL // HD_A
KVH_A = H_A // 4
G_A = H_A // KVH_A
QBLK = 128
HD_B = 64
H_B = D_MODEL // HD_B
KVH_B = H_B // 8
G_B = H_B // KVH_B
WINDOW = 128
D_FF = 5632
CONV_W = 3
PLE_DIM = 256
EPS = 1e-6

kernel_name = "yoco_fox_swa_sink_convffn_step"


def rms_norm(x, g):
    x32 = x.astype(jnp.float32)
    y = x32 * lax.rsqrt(jnp.mean(x32 * x32, axis=-1, keepdims=True) + EPS)
    return (y * g.astype(jnp.float32)).astype(x.dtype)


def alibi_slopes(n):
    return jnp.exp2(-8.0 * jnp.arange(1, n + 1, dtype=jnp.float32) / n)


def fox_project(a, w_in, b_f, q_g, k_g):
    B, T = a.shape[:2]
    z = a @ w_in
    nq, nk = H_A * HD_A, KVH_A * HD_A
    q, k, v, f = jnp.split(z, [nq, nq + nk, nq + 2 * nk], axis=-1)
    q = rms_norm(q.reshape(B, T, H_A, HD_A), q_g)
    k = rms_norm(k.reshape(B, T, KVH_A, HD_A), k_g)
    v = v.reshape(B, T, KVH_A, HD_A)
    logf = jax.nn.log_sigmoid((f + b_f).astype(jnp.float32))
    return q, k, v, logf


def fox_attend(q, k, v, c_q, c_k, q_pos, k_pos):
    B, Q = q.shape[:2]
    qg = q.reshape(B, Q, KVH_A, G_A, HD_A)
    s = jnp.einsum('bqkgd,bskd->bkgqs', qg, k).astype(jnp.float32) * (HD_A ** -0.5)
    cq = c_q.reshape(B, Q, KVH_A, G_A).transpose(0, 2, 3, 1)[..., None]
    ck = c_k.reshape(B, -1, KVH_A, G_A).transpose(0, 2, 3, 1)[..., None, :]
    causal = k_pos[None, :] <= q_pos[:, None]
    s = jnp.where(causal, s + (cq - ck), -jnp.inf)
    p = jax.nn.softmax(s, axis=-1).astype(v.dtype)
    o = jnp.einsum('bkgqs,bskd->bqkgd', p, v)
    return o.reshape(B, Q, H_A * HD_A)


def fox_prompt(q, k, v, logf):
    B, S = q.shape[:2]
    c = jnp.cumsum(logf, axis=1)
    nb = S // QBLK
    qb = q.reshape(B, nb, QBLK, H_A, HD_A).swapaxes(0, 1)
    cb = c.reshape(B, nb, QBLK, H_A).swapaxes(0, 1)
    k_pos = jnp.arange(S)

    def block(args):
        i, qi, ci = args
        q_pos = i * QBLK + jnp.arange(QBLK)
        return fox_attend(qi, k, v, ci, c, q_pos, k_pos)

    o = lax.map(block, (jnp.arange(nb), qb, cb))
    return o.swapaxes(0, 1).reshape(B, S, H_A * HD_A)


def fox_sample(q, k_new, v_new, logf_new, pool_k, pool_v, pool_logf, page_table):
    Bd, T = q.shape[:2]
    L = page_table.shape[1] * PAGE_SIZE
    k_past = pool_k[page_table].reshape(Bd, L, KVH_A, HD_A).astype(k_new.dtype)
    v_past = pool_v[page_table].reshape(Bd, L, KVH_A, HD_A).astype(v_new.dtype)
    lf_past = pool_logf[page_table].reshape(Bd, L, H_A).astype(jnp.float32)
    k = jnp.concatenate([k_past, k_new], axis=1)
    v = jnp.concatenate([v_past, v_new], axis=1)
    c = jnp.cumsum(jnp.concatenate([lf_past, logf_new], axis=1), axis=1)
    q_pos = L + jnp.arange(T)
    k_pos = jnp.arange(L + T)
    return fox_attend(q, k, v, c[:, L:], c, q_pos, k_pos)


def shared_kv(h, g, w_kv, k_g):
    B, T = h.shape[:2]
    k, v = jnp.split(rms_norm(h, g) @ w_kv, 2, axis=-1)
    k = rms_norm(k.reshape(B, T, KVH_B, HD_B), k_g)
    return k, v.reshape(B, T, KVH_B, HD_B)


def swa_attend(q, k, v, sinks, dist, valid):
    B, N, Q = q.shape[:3]
    qg = q.reshape(B, N, Q, KVH_B, G_B, HD_B)
    s = jnp.einsum('bnqkgd,bnskd->bnkgqs', qg, k).astype(jnp.float32) * (HD_B ** -0.5)
    slopes = alibi_slopes(H_B).reshape(KVH_B, G_B, 1, 1)
    s = s - slopes * dist[:, None, None].astype(jnp.float32)
    s = jnp.where(valid[:, None, None], s, -jnp.inf)
    sink = jnp.broadcast_to(sinks.astype(jnp.float32).reshape(KVH_B, G_B, 1, 1), s.shape[:-1] + (1,))
    p = jax.nn.softmax(jnp.concatenate([s, sink], axis=-1), axis=-1)[..., :-1].astype(v.dtype)
    o = jnp.einsum('bnkgqs,bnskd->bnqkgd', p, v)
    return o.reshape(B, N * Q, H_B * HD_B)


def swa_prompt(q, k, v, sinks):
    B, S = q.shape[:2]
    nb = S // WINDOW
    qb = q.reshape(B, nb, WINDOW, H_B, HD_B)
    kb = k.reshape(B, nb, WINDOW, KVH_B, HD_B)
    vb = v.reshape(B, nb, WINDOW, KVH_B, HD_B)
    kband = jnp.concatenate([jnp.concatenate([jnp.zeros_like(kb[:, :1]), kb[:, :-1]], axis=1), kb], axis=2)
    vband = jnp.concatenate([jnp.concatenate([jnp.zeros_like(vb[:, :1]), vb[:, :-1]], axis=1), vb], axis=2)
    blk = jnp.arange(nb)[:, None]
    q_pos = blk * WINDOW + jnp.arange(WINDOW)[None]
    k_pos = (blk - 1) * WINDOW + jnp.arange(2 * WINDOW)[None]
    dist = q_pos[:, :, None] - k_pos[:, None, :]
    valid = (dist >= 0) & (dist < WINDOW) & (k_pos[:, None, :] >= 0)
    return swa_attend(qb, kband, vband, sinks, dist, valid)


def swa_sample(q, k_new, v_new, buf_k, buf_v, sinks, past_len):
    T = q.shape[1]
    W = buf_k.shape[1]
    k = jnp.concatenate([buf_k.astype(k_new.dtype), k_new], axis=1)
    v = jnp.concatenate([buf_v.astype(v_new.dtype), v_new], axis=1)
    q_pos = past_len + jnp.arange(T)
    k_pos = past_len - W + jnp.arange(W + T)
    dist = (q_pos[:, None] - k_pos[None, :])[None]
    valid = (dist >= 0) & (dist < WINDOW)
    return swa_attend(q[:, None], k[:, None], v[:, None], sinks, dist, valid)


def conv_ffn(c, w_in, conv_w, conv_b, w_out, past):
    u = c @ w_in
    T = u.shape[1]
    upad = jnp.concatenate([past.astype(u.dtype), u], axis=1)
    y = conv_b
    for j in range(CONV_W):
        y = y + conv_w[j] * upad[:, j:j + T]
    g, val = jnp.split(y, 2, axis=-1)
    return (jax.nn.gelu(g) * val) @ w_out, upad[:, -(CONV_W - 1):]


def add_ple(h, p, w_proj, w_gate, g):
    gate = jax.nn.sigmoid((rms_norm(h, g) @ w_gate).astype(jnp.float32)).astype(h.dtype)
    return h + (p @ w_proj) * gate


def setup_inputs(seed: int = 0) -> dict:
    key = jax.random.key(seed)
    ks = iter(jax.random.split(key, 40))
    f32 = jnp.float32
    n_pages = PAST_LEN // PAGE_SIZE
    n_pool = (DEC_BATCH * n_pages * 5) // 4
    win_buf = min(WINDOW, PAST_LEN)

    def nrm(shape, scale=1.0):
        return scale * jax.random.normal(next(ks), shape, f32)

    def gain(shape):
        return 1.0 + nrm(shape, 0.05)

    page_table = jax.random.permutation(next(ks), n_pool)[:DEC_BATCH * n_pages]
    page_table = page_table.reshape(DEC_BATCH, n_pages).astype(jnp.int32)
    a_in = H_A * HD_A + 2 * KVH_A * HD_A + H_A
    return {
        "x_prompt": nrm((BATCH, SEQ, D_MODEL)),
        "x_sample": nrm((DEC_BATCH, DEC_SEQ, D_MODEL)),
        "cache_fox_k": nrm((N_LAYERS_A, n_pool, PAGE_SIZE, KVH_A, HD_A)),
        "cache_fox_v": nrm((N_LAYERS_A, n_pool, PAGE_SIZE, KVH_A, HD_A)),
        "cache_fox_logf": jax.nn.log_sigmoid(4.0 + nrm((N_LAYERS_A, n_pool, PAGE_SIZE, H_A))),
        "cache_win_k": nrm((DEC_BATCH, win_buf, KVH_B, HD_B)),
        "cache_win_v": nrm((DEC_BATCH, win_buf, KVH_B, HD_B)),
        "state_conv": nrm((DEPTH, DEC_BATCH, CONV_W - 1, 2 * D_FF)),
        "page_table": page_table,
        "p_prompt": nrm((DEPTH, BATCH, SEQ, PLE_DIM)),
        "p_sample": nrm((DEPTH, DEC_BATCH, DEC_SEQ, PLE_DIM)),
        "norm_attn_g": gain((DEPTH, D_MODEL)),
        "norm_ffn_g": gain((DEPTH, D_MODEL)),
        "norm_ple_g": gain((DEPTH, D_MODEL)),
        "fox_w_in": nrm((N_LAYERS_A, D_MODEL, a_in), D_MODEL ** -0.5),
        "fox_b_f": jax.random.uniform(next(ks), (N_LAYERS_A, H_A), f32, 2.0, 6.0),
        "fox_q_norm_g": gain((N_LAYERS_A, HD_A)),
        "fox_k_norm_g": gain((N_LAYERS_A, HD_A)),
        "fox_w_out": nrm((N_LAYERS_A, H_A * HD_A, D_MODEL), (H_A * HD_A) ** -0.5),
        "kv_norm_g": gain((D_MODEL,)),
        "swa_w_kv": nrm((D_MODEL, 2 * KVH_B * HD_B), D_MODEL ** -0.5),
        "swa_k_norm_g": gain((HD_B,)),
        "swa_w_q": nrm((N_LAYERS_B, D_MODEL, H_B * HD_B), D_MODEL ** -0.5),
        "swa_q_norm_g": gain((N_LAYERS_B, HD_B)),
        "swa_sinks": nrm((N_LAYERS_B, H_B)),
        "swa_w_out": nrm((N_LAYERS_B, H_B * HD_B, D_MODEL), (H_B * HD_B) ** -0.5),
        "ffn_w_in": nrm((DEPTH, D_MODEL, 2 * D_FF), D_MODEL ** -0.5),
        "ffn_conv_w": nrm((DEPTH, CONV_W, 2 * D_FF), CONV_W ** -0.5),
        "ffn_conv_b": nrm((DEPTH, 2 * D_FF), 0.02),
        "ffn_w_out": nrm((DEPTH, D_FF, D_MODEL), D_FF ** -0.5),
        "ple_w_proj": nrm((DEPTH, PLE_DIM, D_MODEL), PLE_DIM ** -0.5),
        "ple_w_gate": nrm((DEPTH, D_MODEL, D_MODEL), D_MODEL ** -0.5),
    }


def reference(x_prompt, x_sample, cache_fox_k, cache_fox_v, cache_fox_logf, cache_win_k, cache_win_v,
              state_conv, page_table, p_prompt, p_sample, norm_attn_g, norm_ffn_g, norm_ple_g,
              fox_w_in, fox_b_f, fox_q_norm_g, fox_k_norm_g, fox_w_out, kv_norm_g, swa_w_kv,
              swa_k_norm_g, swa_w_q, swa_q_norm_g, swa_sinks, swa_w_out, ffn_w_in, ffn_conv_w,
              ffn_conv_b, ffn_w_out, ple_w_proj, ple_w_gate):
    win_buf = cache_win_k.shape[1]
    past_len = page_table.shape[1] * PAGE_SIZE

    def trunk(x, p, fox_attn, swa_attn, conv_past):
        B, T = x.shape[:2]
        h = x
        fk, fv, flf, conv_new = [], [], [], []
        k_sh = v_sh = None
        for i in range(DEPTH):
            a = rms_norm(h, norm_attn_g[i])
            if i < N_LAYERS_A:
                q, k, v, lf = fox_project(a, fox_w_in[i], fox_b_f[i], fox_q_norm_g[i], fox_k_norm_g[i])
                h = h + fox_attn(i, q, k, v, lf) @ fox_w_out[i]
                fk.append(k)
                fv.append(v)
                flf.append(lf)
            else:
                j = i - N_LAYERS_A
                q = rms_norm((a @ swa_w_q[j]).reshape(B, T, H_B, HD_B), swa_q_norm_g[j])
                h = h + swa_attn(q, k_sh, v_sh, swa_sinks[j]) @ swa_w_out[j]
            f, cs = conv_ffn(rms_norm(h, norm_ffn_g[i]), ffn_w_in[i], ffn_conv_w[i], ffn_conv_b[i],
                             ffn_w_out[i], conv_past[i])
            h = h + f
            h = add_ple(h, p[i], ple_w_proj[i], ple_w_gate[i], norm_ple_g[i])
            conv_new.append(cs)
            if i == N_LAYERS_A - 1:
                k_sh, v_sh = shared_kv(h, kv_norm_g, swa_w_kv, swa_k_norm_g)
        return h, jnp.stack(fk), jnp.stack(fv), jnp.stack(flf), k_sh, v_sh, jnp.stack(conv_new)

    conv0 = jnp.zeros((DEPTH, x_prompt.shape[0], CONV_W - 1, 2 * D_FF), x_prompt.dtype)
    y_p, fk_p, fv_p, flf_p, ksh_p, vsh_p, conv_p = trunk(
        x_prompt, p_prompt,
        lambda i, q, k, v, lf: fox_prompt(q, k, v, lf),
        swa_prompt, conv0)
    win_k_p = ksh_p[:, -win_buf:]
    win_v_p = vsh_p[:, -win_buf:]

    y_s, fk_s, fv_s, flf_s, ksh_s, vsh_s, conv_s = trunk(
        x_sample, p_sample,
        lambda i, q, k, v, lf: fox_sample(q, k, v, lf, cache_fox_k[i], cache_fox_v[i],
                                          cache_fox_logf[i], page_table),
        lambda q, k, v, s: swa_sample(q, k, v, cache_win_k, cache_win_v, s, past_len),
        state_conv)
    win_k_s = jnp.concatenate([cache_win_k.astype(ksh_s.dtype), ksh_s], axis=1)[:, -win_buf:]
    win_v_s = jnp.concatenate([cache_win_v.astype(vsh_s.dtype), vsh_s], axis=1)[:, -win_buf:]

    return (y_p, y_s, fk_p, fv_p, flf_p, win_k_p, win_v_p, conv_p,
            fk_s, fv_s, flf_s, win_k_s, win_v_s, conv_s)
```

```python
import functools

import jax
import jax.numpy as jnp
from jax import lax
from jax.experimental import pallas as pl
from jax.experimental.pallas import tpu as pltpu

F32 = jnp.float32
BF16 = jnp.bfloat16

EPS = 1e-6
PAGE_SIZE = 128
WINDOW = 128
CONV_W = 3
NEG = -1e30
LANES = 128
MXU_N = 256
V7X_VMEM_LIMIT = 56 * 1024 * 1024
ROW_TILE = 512
FFN_TILE = 512
HALO = 16
FOX_TQ = 256
PAGES_PER_CHUNK = 8
SCAN_BLOCK = 256
SCAN_ROWS = 128


def _cparams(*sem):
    return pltpu.CompilerParams(dimension_semantics=sem, vmem_limit_bytes=V7X_VMEM_LIMIT)


def _rms(x):
    return x * lax.rsqrt(jnp.mean(x * x, axis=-1, keepdims=True) + EPS)


def _dot(a, b):
    return jnp.dot(a, b, preferred_element_type=F32)


def _dot_nt(a, b):
    return lax.dot_general(a, b, (((1,), (1,)), ((), ())), preferred_element_type=F32)


def _split3(x):
    hi = x.astype(BF16)
    r = x - hi.astype(F32)
    mid = r.astype(BF16)
    lo = (r - mid.astype(F32)).astype(BF16)
    return hi, mid, lo


def _const_spec(shape):
    nd = len(shape)
    return pl.BlockSpec(shape, lambda *_: (0,) * nd)


def _fox_proj_kernel(x_ref, g_ref, wq_ref, wkv_ref, wf_ref, bf_ref, qg_ref, kg_ref,
                     q_ref, k_ref, v_ref, kb_ref, vb_ref, lf_ref, *, nkv):
    a = (_rms(x_ref[...]) * g_ref[...]).astype(BF16)
    qg = qg_ref[...]
    kg = kg_ref[...]
    for c in range(wq_ref.shape[1] // MXU_N):
        z = _dot(a, wq_ref[:, c * MXU_N:(c + 1) * MXU_N])
        for s in range(MXU_N // LANES):
            lo = c * MXU_N + s * LANES
            q_ref[:, lo:lo + LANES] = (_rms(z[:, s * LANES:(s + 1) * LANES]) * qg).astype(BF16)
    for c in range(nkv // MXU_N):
        z = _dot(a, wkv_ref[:, c * MXU_N:(c + 1) * MXU_N])
        for s in range(MXU_N // LANES):
            lo = c * MXU_N + s * LANES
            kn = _rms(z[:, s * LANES:(s + 1) * LANES]) * kg
            k_ref[:, lo:lo + LANES] = kn
            kb_ref[:, lo:lo + LANES] = kn.astype(BF16)
    for c in range(nkv // MXU_N):
        z = _dot(a, wkv_ref[:, nkv + c * MXU_N:nkv + (c + 1) * MXU_N])
        v_ref[:, c * MXU_N:(c + 1) * MXU_N] = z
        vb_ref[:, c * MXU_N:(c + 1) * MXU_N] = z.astype(BF16)
    zf = _dot(a, wf_ref[...]) + bf_ref[...]
    lf_ref[...] = -(jnp.maximum(-zf, 0.0) + jnp.log1p(jnp.exp(-jnp.abs(zf))))


def _fox_proj(x, g, wq, wkv, wf, bf, qg, kg):
    m, d = x.shape
    nq = wq.shape[1]
    nkv = wkv.shape[1] // 2
    tm = min(ROW_TILE, m)
    row = lambda n: pl.BlockSpec((tm, n), lambda i: (i, 0))
    return pl.pallas_call(
        functools.partial(_fox_proj_kernel, nkv=nkv),
        grid=(m // tm,),
        in_specs=[row(d), _const_spec(g.shape), _const_spec(wq.shape), _const_spec(wkv.shape),
                  _const_spec(wf.shape), _const_spec(bf.shape), _const_spec(qg.shape),
                  _const_spec(kg.shape)],
        out_specs=[row(nq), row(nkv), row(nkv), row(nkv), row(nkv), row(LANES)],
        out_shape=[jax.ShapeDtypeStruct((m, nq), BF16), jax.ShapeDtypeStruct((m, nkv), F32),
                   jax.ShapeDtypeStruct((m, nkv), F32), jax.ShapeDtypeStruct((m, nkv), BF16),
                   jax.ShapeDtypeStruct((m, nkv), BF16), jax.ShapeDtypeStruct((m, LANES), F32)],
        compiler_params=_cparams("arbitrary"),
        name="fox_proj",
    )(x, g, wq, wkv, wf, bf, qg, kg)


def _scan_kernel(x_ref, tri_ref, o_ref, *, suffix):
    tr, n = x_ref.shape
    nblk = n // SCAN_BLOCK
    tri = tri_ref[...]
    carry = jnp.zeros((tr, 1), F32)
    order = range(nblk - 1, -1, -1) if suffix else range(nblk)
    for blk in order:
        cols = slice(blk * SCAN_BLOCK, (blk + 1) * SCAN_BLOCK)
        x = x_ref[:, cols]
        r = _dot(jnp.concatenate(_split3(x), axis=0), tri)
        y = r[0:tr] + r[tr:2 * tr] + r[2 * tr:3 * tr] + carry
        if suffix:
            o_ref[:, cols] = y
            carry = y[:, 0:1] + x[:, 0:1]
        else:
            o_ref[:, cols] = -y
            carry = y[:, SCAN_BLOCK - 1:SCAN_BLOCK]


def _scan_lanes(x, suffix):
    rows, n = x.shape
    tr = min(SCAN_ROWS, rows)
    i = lax.broadcasted_iota(jnp.int32, (SCAN_BLOCK, SCAN_BLOCK), 0)
    j = lax.broadcasted_iota(jnp.int32, (SCAN_BLOCK, SCAN_BLOCK), 1)
    tri = jnp.where(i > j if suffix else i <= j, 1.0, 0.0).astype(BF16)
    blk = pl.BlockSpec((tr, n), lambda i: (i, 0))
    return pl.pallas_call(
        functools.partial(_scan_kernel, suffix=suffix),
        grid=(rows // tr,),
        in_specs=[blk, _const_spec(tri.shape)],
        out_specs=blk,
        out_shape=jax.ShapeDtypeStruct((rows, n), F32),
        compiler_params=_cparams("arbitrary"),
        name="logf_suffix" if suffix else "logf_prefix",
    )(x, tri)


def _fox_prompt_kernel(q_ref, k_ref, v_ref, b_ref, o_ref, m_ref, l_ref, acc_ref, *, group, hd):
    i = pl.program_id(2)
    t = q_ref.shape[0]
    m_ref[...] = jnp.full(m_ref.shape, NEG, F32)
    l_ref[...] = jnp.zeros(l_ref.shape, F32)
    acc_ref[...] = jnp.zeros(acc_ref.shape, F32)

    def tile(kt, masked):
        k0 = pl.multiple_of(kt * t, t)
        k = k_ref[pl.ds(k0, t), :]
        v = v_ref[pl.ds(k0, t), :]
        for g in range(group):
            s = _dot_nt(q_ref[:, g * hd:(g + 1) * hd], k) + b_ref[0, 0, g:g + 1, pl.ds(k0, t)]
            if masked:
                r = lax.broadcasted_iota(jnp.int32, (t, t), 0)
                c = lax.broadcasted_iota(jnp.int32, (t, t), 1)
                s = jnp.where(c <= r, s, NEG)
            m_old = m_ref[g]
            m_new = jnp.maximum(m_old, jnp.max(s, axis=-1, keepdims=True))
            p = jnp.exp(s - m_new)
            alpha = jnp.exp(m_old - m_new)
            l_ref[g] = alpha * l_ref[g] + jnp.sum(p, axis=-1, keepdims=True)
            acc_ref[g] = alpha * acc_ref[g] + _dot(p.astype(BF16), v)
            m_ref[g] = m_new

    def body(kt, carry):
        tile(kt, False)
        return carry

    lax.fori_loop(0, i, body, 0)
    tile(i, True)
    for g in range(group):
        o_ref[:, g * hd:(g + 1) * hd] = (acc_ref[g] / l_ref[g]).astype(o_ref.dtype)


def _fox_prompt_attn(q, kb, vb, bias, batch, seq, nkv, hd):
    group = q.shape[1] // (nkv * hd)
    t = FOX_TQ
    nq = seq // t
    return pl.pallas_call(
        functools.partial(_fox_prompt_kernel, group=group, hd=hd),
        grid=(batch, nkv, nq),
        in_specs=[pl.BlockSpec((t, group * hd), lambda b, h, i: (b * nq + i, h)),
                  pl.BlockSpec((seq, hd), lambda b, h, i: (b, h)),
                  pl.BlockSpec((seq, hd), lambda b, h, i: (b, h)),
                  pl.BlockSpec((1, 1, group, seq), lambda b, h, i: (b, h, 0, 0))],
        out_specs=pl.BlockSpec((t, group * hd), lambda b, h, i: (b * nq + i, h)),
        out_shape=jax.ShapeDtypeStruct(q.shape, BF16),
        scratch_shapes=[pltpu.VMEM((group, t, 1), F32), pltpu.VMEM((group, t, 1), F32),
                        pltpu.VMEM((group, t, hd), F32)],
        compiler_params=_cparams("arbitrary", "arbitrary", "arbitrary"),
        name="fox_prompt_attn",
    )(q, kb, vb, bias)


def _fox_sample_kernel(pt_ref, q_ref, kn_ref, vn_ref, bn_ref, e_ref, eye_ref, tri_ref, kpool, vpool, lpool,
                       o_ref, kbuf, vbuf, lbuf, sem, *, nkv, hd, n_chunks, rows_q, group):
    b = pl.program_id(0)
    nb = pl.num_programs(0)
    page_rows = PAGE_SIZE * nkv
    lc = PAGES_PER_CHUNK * PAGE_SIZE
    nh = eye_ref.shape[0]

    def copies(bb, step, slot):
        c = n_chunks - 1 - step
        out = []
        for p in range(PAGES_PER_CHUNK):
            page = pt_ref[bb, c * PAGES_PER_CHUNK + p]
            src = pl.ds(pl.multiple_of(page * page_rows, page_rows), page_rows)
            dst = pl.ds(p * page_rows, page_rows)
            out.append(pltpu.make_async_copy(kpool.at[src, :], kbuf.at[slot, dst, :], sem.at[0, slot]))
            out.append(pltpu.make_async_copy(vpool.at[src, :], vbuf.at[slot, dst, :], sem.at[1, slot]))
            out.append(pltpu.make_async_copy(lpool.at[page], lbuf.at[slot, pl.ds(p * PAGE_SIZE, PAGE_SIZE), :],
                                             sem.at[2, slot]))
        return out

    @pl.when(b == 0)
    def _():
        for cp in copies(0, 0, 0):
            cp.start()

    e = e_ref[...]
    eye = eye_ref[...]
    tri = tri_ref[...]
    q = [q_ref[0, h] for h in range(nkv)]

    def chunk(step, carry):
        slot = lax.rem(b * n_chunks + step, 2)
        nxt_step = lax.rem(step + 1, n_chunks)
        nxt_b = jnp.where(step + 1 == n_chunks, b + 1, b)

        @pl.when(nxt_b < nb)
        def _():
            for cp in copies(nxt_b, nxt_step, 1 - slot):
                cp.start()

        for cp in copies(b, step, slot):
            cp.wait()

        xt = sum(_dot_nt(eye, part) for part in _split3(lbuf[slot]))
        sfx_carry = carry[nkv]
        blocks = []
        for blk in range(lc // SCAN_BLOCK - 1, -1, -1):
            x = xt[:, blk * SCAN_BLOCK:(blk + 1) * SCAN_BLOCK]
            r = _dot(jnp.concatenate(_split3(x), axis=0), tri)
            y = r[0:nh] + r[nh:2 * nh] + r[2 * nh:3 * nh] + sfx_carry
            blocks.insert(0, y)
            sfx_carry = y[:, 0:1] + x[:, 0:1]
        bias = _dot(e, jnp.concatenate(_split3(jnp.concatenate(blocks, axis=1)), axis=0))
        new = []
        for h in range(nkv):
            m_old, l_old, acc_old = carry[h]
            k = kbuf[slot, pl.ds(h, lc, stride=nkv), :].astype(BF16)
            v = vbuf[slot, pl.ds(h, lc, stride=nkv), :].astype(BF16)
            s = _dot_nt(q[h], k) + bias[h * rows_q:(h + 1) * rows_q]
            m_new = jnp.maximum(m_old, jnp.max(s, axis=-1, keepdims=True))
            p = jnp.exp(s - m_new)
            alpha = jnp.exp(m_old - m_new)
            new.append((m_new, alpha * l_old + jnp.sum(p, axis=-1, keepdims=True),
                        alpha * acc_old + _dot(p.astype(BF16), v)))
        return tuple(new) + (sfx_carry,)

    init = tuple((jnp.full((rows_q, 1), NEG, F32), jnp.zeros((rows_q, 1), F32),
                  jnp.zeros((rows_q, hd), F32)) for _ in range(nkv)) + (jnp.zeros((nh, 1), F32),)
    state = lax.fori_loop(0, n_chunks, chunk, init)

    tn = kn_ref.shape[1]
    bias_n = _dot_nt(e, bn_ref[0])
    r = lax.broadcasted_iota(jnp.int32, (rows_q, tn), 0)
    c = lax.broadcasted_iota(jnp.int32, (rows_q, tn), 1)
    ok = c * group <= r
    for h in range(nkv):
        m_old, l_old, acc_old = state[h]
        k = kn_ref[0, :, h * hd:(h + 1) * hd].astype(BF16)
        v = vn_ref[0, :, h * hd:(h + 1) * hd].astype(BF16)
        s = jnp.where(ok, _dot_nt(q[h], k) + bias_n[h * rows_q:(h + 1) * rows_q], NEG)
        m_new = jnp.maximum(m_old, jnp.max(s, axis=-1, keepdims=True))
        p = jnp.exp(s - m_new)
        alpha = jnp.exp(m_old - m_new)
        l_new = alpha * l_old + jnp.sum(p, axis=-1, keepdims=True)
        acc = alpha * acc_old + _dot(p.astype(BF16), v)
        o_ref[0, h] = (acc / l_new).astype(o_ref.dtype)


def _fox_sample_attn(page_table, q, k_new, v_new, bias_new3, emat, kpool, vpool, lpool, nkv, hd, group):
    nb, _, rows_q, _ = q.shape
    n_pages = page_table.shape[1]
    nh = lpool.shape[-1]
    n_chunks = n_pages // PAGES_PER_CHUNK
    buf_rows = PAGES_PER_CHUNK * PAGE_SIZE * nkv
    tn = k_new.shape[1]
    eye = jnp.eye(nh, dtype=BF16)
    i = lax.broadcasted_iota(jnp.int32, (SCAN_BLOCK, SCAN_BLOCK), 0)
    j = lax.broadcasted_iota(jnp.int32, (SCAN_BLOCK, SCAN_BLOCK), 1)
    tri = jnp.where(i > j, 1.0, 0.0).astype(BF16)
    const = lambda a: pl.BlockSpec(a.shape, lambda b, pt: (0,) * a.ndim)
    grid_spec = pltpu.PrefetchScalarGridSpec(
        num_scalar_prefetch=1,
        grid=(nb,),
        in_specs=[pl.BlockSpec((1, nkv, rows_q, hd), lambda b, pt: (b, 0, 0, 0)),
                  pl.BlockSpec((1, tn, nkv * hd), lambda b, pt: (b, 0, 0)),
                  pl.BlockSpec((1, tn, nkv * hd), lambda b, pt: (b, 0, 0)),
                  pl.BlockSpec((1,) + bias_new3.shape[1:], lambda b, pt: (b, 0, 0)),
                  const(emat), const(eye), const(tri),
                  pl.BlockSpec(memory_space=pl.ANY),
                  pl.BlockSpec(memory_space=pl.ANY),
                  pl.BlockSpec(memory_space=pl.ANY)],
        out_specs=pl.BlockSpec((1, nkv, rows_q, hd), lambda b, pt: (b, 0, 0, 0)),
        scratch_shapes=[pltpu.VMEM((2, buf_rows, hd), F32), pltpu.VMEM((2, buf_rows, hd), F32),
                        pltpu.VMEM((2, PAGES_PER_CHUNK * PAGE_SIZE, nh), F32),
                        pltpu.SemaphoreType.DMA((3, 2))],
    )
    return pl.pallas_call(
        functools.partial(_fox_sample_kernel, nkv=nkv, hd=hd, n_chunks=n_chunks, rows_q=rows_q, group=group),
        grid_spec=grid_spec,
        out_shape=jax.ShapeDtypeStruct(q.shape, BF16),
        compiler_params=_cparams("arbitrary"),
        name="fox_sample_attn",
    )(page_table, q, k_new, v_new, bias_new3, emat, eye, tri, kpool, vpool, lpool)


def _out_proj_kernel(o_ref, h_ref, w_ref, g_ref, h1_ref, c_ref):
    h1 = h_ref[...] + _dot(o_ref[...], w_ref[...])
    h1_ref[...] = h1
    c_ref[...] = (_rms(h1) * g_ref[...]).astype(BF16)


def _out_proj(o, h, w, g):
    m, d = h.shape
    tm = min(ROW_TILE, m)
    row = lambda n: pl.BlockSpec((tm, n), lambda i: (i, 0))
    return pl.pallas_call(
        _out_proj_kernel,
        grid=(m // tm,),
        in_specs=[row(o.shape[1]), row(d), _const_spec(w.shape), _const_spec(g.shape)],
        out_specs=[row(d), row(d)],
        out_shape=[jax.ShapeDtypeStruct((m, d), F32), jax.ShapeDtypeStruct((m, d), BF16)],
        compiler_params=_cparams("arbitrary"),
        name="attn_out_proj",
    )(o, h, w, g)


def _conv_gate(ug, uv, cwg_ref, cwv_ref, cbg_ref, cbv_ref, taps):
    def conv(u_taps, cw_ref, cb_ref):
        y = cb_ref[...]
        for j in range(CONV_W):
            y = y + cw_ref[j:j + 1, :] * u_taps[j]
        return y
    g = conv(taps(ug), cwg_ref, cbg_ref)
    v = conv(taps(uv), cwv_ref, cbv_ref)
    return (jax.nn.gelu(g, approximate=True) * v).astype(BF16)


def _ffn_epilogue(j, h_ref, gn_ref, acc_ref, h2_ref, n_ref):
    @pl.when(j == pl.num_programs(1) - 1)
    def _():
        h2 = h_ref[...] + acc_ref[...]
        h2_ref[...] = h2
        n_ref[...] = (_rms(h2) * gn_ref[...]).astype(BF16)


def _ffn_prompt_kernel(c_ref, halo_ref, wg_ref, wv_ref, cwg_ref, cwv_ref, cbg_ref, cbv_ref, wo_ref,
                       h_ref, gn_ref, h2_ref, n_ref, acc_ref, *, tiles_per_seq):
    i, j = pl.program_id(0), pl.program_id(1)
    tm = c_ref.shape[0]

    @pl.when(j == 0)
    def _():
        acc_ref[...] = jnp.zeros(acc_ref.shape, F32)

    halo = halo_ref[...]
    halo = jnp.where(lax.rem(i, tiles_per_seq) == 0, jnp.zeros_like(halo), halo)
    x = jnp.concatenate([halo, c_ref[...]], axis=0)
    taps = lambda u: [u[HALO - (CONV_W - 1) + k:HALO - (CONV_W - 1) + k + tm] for k in range(CONV_W)]
    act = _conv_gate(_dot(x, wg_ref[...]), _dot(x, wv_ref[...]), cwg_ref, cwv_ref, cbg_ref, cbv_ref, taps)
    acc_ref[...] += _dot(act, wo_ref[...])
    _ffn_epilogue(j, h_ref, gn_ref, acc_ref, h2_ref, n_ref)


def _ffn_sample_kernel(c_ref, pg_ref, pv_ref, wg_ref, wv_ref, cwg_ref, cwv_ref, cbg_ref, cbv_ref, wo_ref,
                       h_ref, gn_ref, h2_ref, n_ref, acc_ref):
    j = pl.program_id(1)
    nb = pg_ref.shape[1]
    nt = c_ref.shape[0] // nb

    @pl.when(j == 0)
    def _():
        acc_ref[...] = jnp.zeros(acc_ref.shape, F32)

    x = c_ref[...]

    def taps_of(p_ref):
        def taps(u):
            slabs = [p_ref[k] for k in range(CONV_W - 1)] + [u[t * nb:(t + 1) * nb] for t in range(nt)]
            return [jnp.concatenate(slabs[k:k + nt], axis=0) for k in range(CONV_W)]
        return taps

    def conv(u, p_ref, cw_ref, cb_ref):
        y = cb_ref[...]
        for k, tap in enumerate(taps_of(p_ref)(u)):
            y = y + cw_ref[k:k + 1, :] * tap
        return y

    g = conv(_dot(x, wg_ref[...]), pg_ref, cwg_ref, cbg_ref)
    v = conv(_dot(x, wv_ref[...]), pv_ref, cwv_ref, cbv_ref)
    act = (jax.nn.gelu(g, approximate=True) * v).astype(BF16)
    acc_ref[...] += _dot(act, wo_ref[...])
    _ffn_epilogue(j, h_ref, gn_ref, acc_ref, h2_ref, n_ref)


def _conv_ffn(c, h, w_in, conv_w, conv_b, w_out, g_next, past, tiles_per_seq):
    m, d = h.shape
    dff = w_out.shape[0]
    tf = FFN_TILE
    nf = dff // tf
    tm = min(ROW_TILE, m)
    wspecs = [pl.BlockSpec((d, tf), lambda i, j: (0, j)), pl.BlockSpec((d, tf), lambda i, j: (0, nf + j)),
              pl.BlockSpec((CONV_W, tf), lambda i, j: (0, j)), pl.BlockSpec((CONV_W, tf), lambda i, j: (0, nf + j)),
              pl.BlockSpec((1, tf), lambda i, j: (0, j)), pl.BlockSpec((1, tf), lambda i, j: (0, nf + j)),
              pl.BlockSpec((tf, d), lambda i, j: (j, 0))]
    wargs = [w_in, w_in, conv_w, conv_w, conv_b, conv_b, w_out]
    row = pl.BlockSpec((tm, d), lambda i, j: (i, 0))
    tail_specs = [row, pl.BlockSpec((1, d), lambda i, j: (0, 0))]
    if past is None:
        per = tm // HALO
        head_specs = [row, pl.BlockSpec((HALO, d), lambda i, j: (jnp.maximum(i * per - 1, 0), 0))]
        head_args = [c, c]
        kern = functools.partial(_ffn_prompt_kernel, tiles_per_seq=tiles_per_seq)
    else:
        nb = past.shape[1]
        head_specs = [row, pl.BlockSpec((CONV_W - 1, nb, tf), lambda i, j: (0, 0, j)),
                      pl.BlockSpec((CONV_W - 1, nb, tf), lambda i, j: (0, 0, nf + j))]
        head_args = [c, past, past]
        kern = _ffn_sample_kernel
    return pl.pallas_call(
        kern,
        grid=(m // tm, nf),
        in_specs=head_specs + wspecs + tail_specs,
        out_specs=[row, row],
        out_shape=[jax.ShapeDtypeStruct((m, d), F32), jax.ShapeDtypeStruct((m, d), BF16)],
        scratch_shapes=[pltpu.VMEM((tm, d), F32)],
        compiler_params=_cparams("arbitrary", "arbitrary"),
        name="conv_ffn_prompt" if past is None else "conv_ffn_sample",
    )(*head_args, *wargs, h, g_next)


def _matmul_kernel(x_ref, w_ref, o_ref):
    o_ref[...] = _dot(x_ref[...], w_ref[...])


def _matmul(x, w, tn=1024):
    m, k = x.shape
    n = w.shape[1]
    return pl.pallas_call(
        _matmul_kernel,
        grid=(n // tn,),
        in_specs=[pl.BlockSpec((m, k), lambda j: (0, 0)), pl.BlockSpec((k, tn), lambda j: (0, j))],
        out_specs=pl.BlockSpec((m, tn), lambda j: (0, j)),
        out_shape=jax.ShapeDtypeStruct((m, n), F32),
        compiler_params=_cparams("arbitrary"),
        name="conv_state_rows",
    )(x, w)


def _ple_kernel(h_ref, n_ref, p_ref, wp_ref, wg_ref, *rest, n_norms):
    g_refs = rest[:n_norms]
    h3_ref = rest[n_norms]
    a_refs = rest[n_norms + 1:]
    n = n_ref[...]
    p = p_ref[...].astype(BF16)
    d = h_ref.shape[1]
    tn = 512
    for c in range(d // tn):
        cs = slice(c * tn, (c + 1) * tn)
        gate = jax.nn.sigmoid(_dot(n, wg_ref[:, cs]))
        h3_ref[:, cs] = h_ref[:, cs] + _dot(p, wp_ref[:, cs]) * gate
    if n_norms:
        xhat = _rms(h3_ref[...])
        for g_ref, a_ref in zip(g_refs, a_refs):
            a_ref[...] = (xhat * g_ref[...]).astype(BF16)


def _ple(h, n, p, wp, wg, gains):
    m, d = h.shape
    tm = min(ROW_TILE, m)
    row = lambda w: pl.BlockSpec((tm, w), lambda i: (i, 0))
    k = len(gains)
    outs = pl.pallas_call(
        functools.partial(_ple_kernel, n_norms=k),
        grid=(m // tm,),
        in_specs=[row(d), row(d), row(p.shape[1]), _const_spec(wp.shape), _const_spec(wg.shape)]
                 + [_const_spec(g.shape) for g in gains],
        out_specs=[row(d)] * (k + 1),
        out_shape=[jax.ShapeDtypeStruct((m, d), F32)] + [jax.ShapeDtypeStruct((m, d), BF16)] * k,
        compiler_params=_cparams("arbitrary"),
        name="ple",
    )(h, n, p, wp, wg, *gains)
    return outs


def _swa_proj_kernel(a_ref, akv_ref, wq_ref, wkv_ref, qg_ref, kg_ref, q_ref, k_ref, v_ref, *, nkv, hd):
    a = a_ref[...]
    qg = qg_ref[...]
    lane = lax.broadcasted_iota(jnp.int32, (a.shape[0], LANES), 1)
    low = lane < hd
    for c in range(wq_ref.shape[1] // MXU_N):
        z = _dot(a, wq_ref[:, c * MXU_N:(c + 1) * MXU_N])
        for s in range(MXU_N // LANES):
            zz = z[:, s * LANES:(s + 1) * LANES]
            sq = zz * zz
            s_lo = jnp.sum(jnp.where(low, sq, 0.0), axis=-1, keepdims=True)
            s_hi = jnp.sum(jnp.where(low, 0.0, sq), axis=-1, keepdims=True)
            inv = jnp.where(low, lax.rsqrt(s_lo / hd + EPS), lax.rsqrt(s_hi / hd + EPS))
            lo = c * MXU_N + s * LANES
            q_ref[:, lo:lo + LANES] = (zz * inv * qg).astype(BF16)
    akv = akv_ref[...]
    kg = kg_ref[...]
    for c in range(nkv // MXU_N):
        z = _dot(akv, wkv_ref[:, c * MXU_N:(c + 1) * MXU_N])
        for s in range(MXU_N // LANES):
            lo = c * MXU_N + s * LANES
            k_ref[:, lo:lo + LANES] = _rms(z[:, s * LANES:(s + 1) * LANES]) * kg
    for c in range(nkv // MXU_N):
        v_ref[:, c * MXU_N:(c + 1) * MXU_N] = _dot(akv, wkv_ref[:, nkv + c * MXU_N:nkv + (c + 1) * MXU_N])


def _swa_proj(a, akv, wq, wkv_dup, qg, kg, hd):
    m, d = a.shape
    nkv = wkv_dup.shape[1] // 2
    tm = min(ROW_TILE, m)
    row = lambda n: pl.BlockSpec((tm, n), lambda i: (i, 0))
    return pl.pallas_call(
        functools.partial(_swa_proj_kernel, nkv=nkv, hd=hd),
        grid=(m // tm,),
        in_specs=[row(d), row(d), _const_spec(wq.shape), _const_spec(wkv_dup.shape),
                  _const_spec(qg.shape), _const_spec(kg.shape)],
        out_specs=[row(wq.shape[1]), row(nkv), row(nkv)],
        out_shape=[jax.ShapeDtypeStruct((m, wq.shape[1]), BF16), jax.ShapeDtypeStruct((m, nkv), F32),
                   jax.ShapeDtypeStruct((m, nkv), F32)],
        compiler_params=_cparams("arbitrary"),
        name="swa_proj",
    )(a, akv, wq, wkv_dup, qg, kg)


def _swa_prompt_kernel(slope_ref, sink_ref, q_ref, kp_ref, kc_ref, vp_ref, vc_ref, o_ref, *, nkv, hd):
    i = pl.program_id(1)
    w = q_ref.shape[0]
    pairs = q_ref.shape[1] // (nkv * LANES)
    lane = lax.broadcasted_iota(jnp.int32, (w, LANES), 1)
    low = lane < hd
    r = lax.broadcasted_iota(jnp.int32, (2 * w, 2 * w), 0)
    c = lax.broadcasted_iota(jnp.int32, (2 * w, 2 * w), 1)
    qpos = jnp.where(r < w, r, r - w) + w
    dist = qpos - c
    valid = (dist >= 0) & (dist < w) & ((c >= w) | (i > 0))
    distf = dist.astype(F32)
    top = r < w
    rcol = lax.broadcasted_iota(jnp.int32, (2 * w, 1), 0) < w
    for h in range(nkv):
        k = jnp.concatenate([kp_ref[:, h * LANES:(h + 1) * LANES], kc_ref[:, h * LANES:(h + 1) * LANES]],
                            axis=0).astype(BF16)
        v = jnp.concatenate([vp_ref[:, h * LANES:(h + 1) * LANES], vc_ref[:, h * LANES:(h + 1) * LANES]],
                            axis=0).astype(BF16)
        for pr in range(pairs):
            col = (h * pairs + pr) * LANES
            h0 = 2 * (h * pairs + pr)
            qp = q_ref[:, col:col + LANES]
            rows = jnp.concatenate([jnp.where(low, qp, jnp.zeros_like(qp)),
                                    jnp.where(low, jnp.zeros_like(qp), qp)], axis=0)
            slope = jnp.where(top, slope_ref[h0], slope_ref[h0 + 1])
            s = jnp.where(valid, _dot_nt(rows, k) - slope * distf, NEG)
            sink = jnp.where(rcol, sink_ref[h0], sink_ref[h0 + 1])
            m = jnp.maximum(jnp.max(s, axis=-1, keepdims=True), sink)
            p = jnp.exp(s - m)
            den = jnp.sum(p, axis=-1, keepdims=True) + jnp.exp(sink - m)
            o = _dot(p.astype(BF16), v) / den
            o_ref[:, col:col + LANES] = jnp.where(low, o[:w], o[w:]).astype(o_ref.dtype)


def _swa_prompt_attn(q, kdup, vdup, slopes, sinks, batch, seq, nkv, hd):
    nblk = seq // WINDOW
    dq = q.shape[1]
    dk = kdup.shape[1]
    cur = lambda b, i: (b * nblk + i, 0)
    prev = lambda b, i: (b * nblk + jnp.maximum(i - 1, 0), 0)
    smem = pl.BlockSpec(memory_space=pltpu.SMEM)
    return pl.pallas_call(
        functools.partial(_swa_prompt_kernel, nkv=nkv, hd=hd),
        grid=(batch, nblk),
        in_specs=[smem, smem, pl.BlockSpec((WINDOW, dq), cur), pl.BlockSpec((WINDOW, dk), prev),
                  pl.BlockSpec((WINDOW, dk), cur), pl.BlockSpec((WINDOW, dk), prev),
                  pl.BlockSpec((WINDOW, dk), cur)],
        out_specs=pl.BlockSpec((WINDOW, dq), cur),
        out_shape=jax.ShapeDtypeStruct(q.shape, BF16),
        compiler_params=_cparams("arbitrary", "arbitrary"),
        name="swa_prompt_attn",
    )(slopes, sinks, q, kdup, kdup, vdup, vdup)


def _swa_sample_kernel(q_ref, kc_ref, vc_ref, kn_ref, vn_ref, slope_ref, sink_ref, o_ref, *, heads):
    q = q_ref[0]
    rows = q.shape[0]
    w = kc_ref.shape[1]
    tn = kn_ref.shape[1]
    slope = slope_ref[...]
    sink = sink_ref[...]
    t_p = lax.broadcasted_iota(jnp.int32, (rows, w), 0) // heads
    j_p = lax.broadcasted_iota(jnp.int32, (rows, w), 1)
    dist_p = t_p + w - j_p
    s_p = jnp.where(dist_p < WINDOW,_dot_nt(q, kc_ref[0].astype(BF16)) - slope * dist_p.astype(F32), NEG)
    t_n = lax.broadcasted_iota(jnp.int32, (rows, tn), 0) // heads
    j_n = lax.broadcasted_iota(jnp.int32, (rows, tn), 1)
    dist_n = t_n - j_n
    s_n = jnp.where(dist_n >= 0, _dot_nt(q, kn_ref[0].astype(BF16)) - slope * dist_n.astype(F32), NEG)
    m = jnp.maximum(jnp.maximum(jnp.max(s_p, axis=-1, keepdims=True), jnp.max(s_n, axis=-1, keepdims=True)), sink)
    p_p = jnp.exp(s_p - m)
    p_n = jnp.exp(s_n - m)
    den = jnp.sum(p_p, axis=-1, keepdims=True) + jnp.sum(p_n, axis=-1, keepdims=True) + jnp.exp(sink - m)
    o = _dot(p_p.astype(BF16), vc_ref[0].astype(BF16)) + _dot(p_n.astype(BF16), vn_ref[0].astype(BF16))
    o_ref[0] = o / den


def _swa_sample_attn(q, kc, vc, kn, vn, slope_col, sink_col, heads):
    nb, rows, dk = q.shape
    blk = lambda a: pl.BlockSpec((1,) + a.shape[1:], lambda b: (b, 0, 0))
    return pl.pallas_call(
        functools.partial(_swa_sample_kernel, heads=heads),
        grid=(nb,),
        in_specs=[blk(q), blk(kc), blk(vc), blk(kn), blk(vn), _const_spec(slope_col.shape),
                  _const_spec(sink_col.shape)],
        out_specs=blk(q),
        out_shape=jax.ShapeDtypeStruct((nb, rows, dk), F32),
        compiler_params=_cparams("arbitrary"),
        name="swa_sample_attn",
    )(q, kc, vc, kn, vn, slope_col, sink_col)


def kernel(x_prompt, x_sample, cache_fox_k, cache_fox_v, cache_fox_logf, cache_win_k, cache_win_v, state_conv, page_table, p_prompt, p_sample, norm_attn_g, norm_ffn_g, norm_ple_g, fox_w_in, fox_b_f, fox_q_norm_g, fox_k_norm_g, fox_w_out, kv_norm_g, swa_w_kv, swa_k_norm_g, swa_w_q, swa_q_norm_g, swa_sinks, swa_w_out, ffn_w_in, ffn_conv_w, ffn_conv_b, ffn_w_out, ple_w_proj, ple_w_gate):
    B, S, D = x_prompt.shape
    NB, T, _ = x_sample.shape
    HD_A = fox_q_norm_g.shape[-1]
    H_A = fox_b_f.shape[-1]
    KVH_A = cache_fox_k.shape[3]
    G_A = H_A // KVH_A
    HD_B = swa_q_norm_g.shape[-1]
    H_B = swa_sinks.shape[-1]
    KVH_B = cache_win_k.shape[2]
    G_B = H_B // KVH_B
    WIN = cache_win_k.shape[1]
    L = page_table.shape[1] * PAGE_SIZE
    DFF = ffn_w_out.shape[1]
    NQ_A, NK_A = H_A * HD_A, KVH_A * HD_A
    row2 = lambda g: g.reshape(1, -1)

    w_in0 = fox_w_in[0]
    wq_a = w_in0[:, :NQ_A].astype(BF16)
    wkv_a = w_in0[:, NQ_A:NQ_A + 2 * NK_A].astype(BF16)
    wf_a = jnp.pad(w_in0[:, NQ_A + 2 * NK_A:], ((0, 0), (0, LANES - H_A))).astype(BF16)
    bf_a = jnp.pad(fox_b_f[0], (0, LANES - H_A)).reshape(1, LANES)
    qg_a = row2(fox_q_norm_g[0] * (HD_A ** -0.5))
    kg_a = row2(fox_k_norm_g[0])
    wo_a = fox_w_out[0].astype(BF16)
    wq_b = swa_w_q[0].astype(BF16)
    dup = lambda w: jnp.repeat(w.reshape(D, KVH_B, 1, HD_B), LANES // HD_B, axis=2).reshape(D, KVH_B * LANES)
    wk_b, wv_b = jnp.split(swa_w_kv, 2, axis=-1)
    wkv_b = jnp.concatenate([dup(wk_b), dup(wv_b)], axis=1).astype(BF16)
    qg_b = row2(jnp.tile(swa_q_norm_g[0] * (HD_B ** -0.5), LANES // HD_B))
    kg_b = row2(jnp.tile(swa_k_norm_g, LANES // HD_B))
    wo_b = swa_w_out[0].astype(BF16)
    ffn_wi = ffn_w_in.astype(BF16)
    ffn_wo = ffn_w_out.astype(BF16)
    ple_wp = ple_w_proj.astype(BF16)
    ple_wg = ple_w_gate.astype(BF16)
    slopes = jnp.exp2(-8.0 * jnp.arange(1, H_B + 1, dtype=F32) / H_B)
    sinks = swa_sinks[0]

    xp = x_prompt.reshape(B * S, D)
    xs = x_sample.transpose(1, 0, 2).reshape(T * NB, D)
    pp = p_prompt.reshape(2, B * S, -1)
    ps = p_sample.transpose(0, 2, 1, 3).reshape(2, T * NB, -1)
    past = state_conv.transpose(0, 2, 1, 3)

    g0 = row2(norm_attn_g[0])
    q_p, k_p, v_p, kb_p, vb_p, lf_p = _fox_proj(xp, g0, wq_a, wkv_a, wf_a, bf_a, qg_a, kg_a)
    q_s, k_s, v_s, _, _, lf_s = _fox_proj(xs, g0, wq_a, wkv_a, wf_a, bf_a, qg_a, kg_a)
    lf_p = lf_p[:, :H_A]
    lf_s = lf_s[:, :H_A]

    bias_p = _scan_lanes(lf_p.reshape(B, S, H_A).transpose(0, 2, 1).reshape(B * H_A, S), suffix=False)
    o_p = _fox_prompt_attn(q_p, kb_p, vb_p, bias_p.reshape(B, KVH_A, G_A, S), B, S, KVH_A, HD_A)

    tn = 16
    lf_new = lf_s.reshape(T, NB, H_A).transpose(1, 2, 0).reshape(NB * H_A, T)
    bias_new = _scan_lanes(jnp.pad(lf_new, ((0, 0), (0, SCAN_BLOCK - T))), suffix=False)[:, :tn]
    bias_new3 = jnp.concatenate(_split3(bias_new.reshape(NB, H_A, tn).transpose(0, 2, 1)), axis=-1)
    rows_q = T * G_A
    head_of_row = (jnp.arange(KVH_A)[:, None] * G_A + jnp.arange(rows_q)[None, :] % G_A).reshape(-1)
    emat = jnp.tile(jax.nn.one_hot(head_of_row, H_A, dtype=BF16), (1, 3))
    qs_b = q_s.reshape(T, NB, KVH_A, G_A, HD_A).transpose(1, 2, 0, 3, 4).reshape(NB, KVH_A, rows_q, HD_A)
    pad_new = lambda a: jnp.pad(a.reshape(T, NB, NK_A).transpose(1, 0, 2), ((0, 0), (0, tn - T), (0, 0)))
    kpool = cache_fox_k.reshape(-1, HD_A)
    vpool = cache_fox_v.reshape(-1, HD_A)
    o_s = _fox_sample_attn(page_table, qs_b, pad_new(k_s), pad_new(v_s), bias_new3, emat, kpool, vpool,
                           cache_fox_logf.reshape(cache_fox_logf.shape[1:]), KVH_A, HD_A, G_A)
    o_s = o_s.reshape(NB, KVH_A, T, G_A, HD_A).transpose(2, 0, 1, 3, 4).reshape(T * NB, NQ_A)

    gf0 = row2(norm_ffn_g[0])
    h_p, c_p = _out_proj(o_p, xp, wo_a, gf0)
    h_s, c_s = _out_proj(o_s, xs, wo_a, gf0)

    gp0 = row2(norm_ple_g[0])
    tiles_per_seq = S // ROW_TILE
    h_p, n_p = _conv_ffn(c_p, h_p, ffn_wi[0], ffn_conv_w[0], row2(ffn_conv_b[0]), ffn_wo[0], gp0, None, tiles_per_seq)
    h_s, n_s = _conv_ffn(c_s, h_s, ffn_wi[0], ffn_conv_w[0], row2(ffn_conv_b[0]), ffn_wo[0], gp0, past[0], 1)
    tail_p = lambda c: c.reshape(B, S, D)[:, S - (CONV_W - 1):].reshape(B * (CONV_W - 1), D)
    tail_rows = lambda cp, cs: jnp.concatenate(
        [jnp.pad(tail_p(cp), ((0, HALO - B * (CONV_W - 1)), (0, 0))), cs[(T - (CONV_W - 1)) * NB:]], axis=0)
    u0 = _matmul(tail_rows(c_p, c_s), ffn_wi[0])

    gq1, gkv = row2(norm_attn_g[1]), row2(kv_norm_g)
    h_p, a_p, akv_p = _ple(h_p, n_p, pp[0], ple_wp[0], ple_wg[0], [gq1, gkv])
    h_s, a_s, akv_s = _ple(h_s, n_s, ps[0], ple_wp[0], ple_wg[0], [gq1, gkv])

    q1_p, kd_p, vd_p = _swa_proj(a_p, akv_p, wq_b, wkv_b, qg_b, kg_b, HD_B)
    q1_s, kd_s, vd_s = _swa_proj(a_s, akv_s, wq_b, wkv_b, qg_b, kg_b, HD_B)
    undup = lambda a: a.reshape(a.shape[0], KVH_B, LANES // HD_B, HD_B)[:, :, 0]
    o1_p = _swa_prompt_attn(q1_p, kd_p, vd_p, slopes, sinks, B, S, KVH_B, HD_B)

    ksh_s = undup(kd_s).reshape(T, NB, KVH_B * HD_B).transpose(1, 0, 2)
    vsh_s = undup(vd_s).reshape(T, NB, KVH_B * HD_B).transpose(1, 0, 2)
    q1h = q1_s.reshape(T, NB, H_B, 1, HD_B).transpose(1, 0, 2, 3, 4)
    slab = (jnp.arange(H_B)[:, None] // G_B == jnp.arange(KVH_B)[None, :])[None, None, :, :, None]
    q1x = jnp.where(slab, q1h, jnp.zeros((), BF16)).reshape(NB, T * H_B, KVH_B * HD_B)
    padn = lambda a: jnp.pad(a, ((0, 0), (0, tn - T), (0, 0)))
    slope_col = jnp.tile(slopes, T).reshape(T * H_B, 1)
    sink_col = jnp.tile(sinks, T).reshape(T * H_B, 1)
    o1x = _swa_sample_attn(q1x, cache_win_k.reshape(NB, WIN, -1), cache_win_v.reshape(NB, WIN, -1),
                           padn(ksh_s), padn(vsh_s), slope_col, sink_col, H_B)
    o1x = o1x.reshape(NB, T, H_B, KVH_B, HD_B)
    o1_s = jnp.sum(jnp.where(slab, o1x, 0.0), axis=3)
    o1_s = o1_s.transpose(1, 0, 2, 3).reshape(T * NB, H_B * HD_B).astype(BF16)

    gf1 = row2(norm_ffn_g[1])
    h_p, c_p = _out_proj(o1_p, h_p, wo_b, gf1)
    h_s, c_s = _out_proj(o1_s, h_s, wo_b, gf1)

    gp1 = row2(norm_ple_g[1])
    h_p, n_p = _conv_ffn(c_p, h_p, ffn_wi[1], ffn_conv_w[1], row2(ffn_conv_b[1]), ffn_wo[1], gp1, None, tiles_per_seq)
    h_s, n_s = _conv_ffn(c_s, h_s, ffn_wi[1], ffn_conv_w[1], row2(ffn_conv_b[1]), ffn_wo[1], gp1, past[1], 1)
    u1 = _matmul(tail_rows(c_p, c_s), ffn_wi[1])

    (y_p,) = _ple(h_p, n_p, pp[1], ple_wp[1], ple_wg[1], [])
    (y_s,) = _ple(h_s, n_s, ps[1], ple_wp[1], ple_wg[1], [])

    to_bt = lambda a, *tail: a.reshape((T, NB) + tail).swapaxes(0, 1)
    y_prompt = y_p.reshape(B, S, D)
    y_sample = to_bt(y_s, D)
    fk_p = k_p.reshape(1, B, S, KVH_A, HD_A)
    fv_p = v_p.reshape(1, B, S, KVH_A, HD_A)
    flf_p = lf_p.reshape(1, B, S, H_A)
    ksh_p = undup(kd_p).reshape(B, S, KVH_B, HD_B)
    vsh_p = undup(vd_p).reshape(B, S, KVH_B, HD_B)
    win_k_p = ksh_p[:, S - WIN:]
    win_v_p = vsh_p[:, S - WIN:]
    nrow_p = B * (CONV_W - 1)
    conv_p = jnp.stack([u[:nrow_p].reshape(B, CONV_W - 1, 2 * DFF) for u in (u0, u1)])
    conv_s = jnp.stack([u[HALO:].reshape(CONV_W - 1, NB, 2 * DFF).swapaxes(0, 1) for u in (u0, u1)])
    fk_s = to_bt(k_s, KVH_A, HD_A)[None]
    fv_s = to_bt(v_s, KVH_A, HD_A)[None]
    flf_s = to_bt(lf_s, H_A)[None]
    win_k_s = jnp.concatenate([cache_win_k, ksh_s.reshape(NB, T, KVH_B, HD_B)], axis=1)[:, -WIN:]
    win_v_s = jnp.concatenate([cache_win_v, vsh_s.reshape(NB, T, KVH_B, HD_B)], axis=1)[:, -WIN:]
    return (y_prompt, y_sample, fk_p, fv_p, flf_p, win_k_p, win_v_p, conv_p,
            fk_s, fv_s, flf_s, win_k_s, win_v_s, conv_s)
```

```python
import functools

import jax
import jax.numpy as jnp
from jax import lax
from jax.experimental import pallas as pl
from jax.experimental.pallas import tpu as pltpu

F32 = jnp.float32
BF16 = jnp.bfloat16

EPS = 1e-6
PAGE_SIZE = 128
WINDOW = 128
CONV_W = 3
NEG = -1e30
LANES = 128
MXU_N = 256
V7X_VMEM_LIMIT = 56 * 1024 * 1024
ROW_TILE = 512
FFN_TILE = 512
HALO = 16
FOX_TQ = 512
PAGES_PER_CHUNK = 16
SCAN_BLOCK = 256
SCAN_ROWS = 128


def _cparams(*sem):
    return pltpu.CompilerParams(dimension_semantics=sem, vmem_limit_bytes=V7X_VMEM_LIMIT)


def _rms(x):
    return x * lax.rsqrt(jnp.mean(x * x, axis=-1, keepdims=True) + EPS)


def _dot(a, b):
    return jnp.dot(a, b, preferred_element_type=F32)


def _dot_nt(a, b):
    return lax.dot_general(a, b, (((1,), (1,)), ((), ())), preferred_element_type=F32)


def _split3(x):
    hi = x.astype(BF16)
    r = x - hi.astype(F32)
    mid = r.astype(BF16)
    lo = (r - mid.astype(F32)).astype(BF16)
    return hi, mid, lo


def _const_spec(shape):
    nd = len(shape)
    return pl.BlockSpec(shape, lambda *_: (0,) * nd)


def _fox_proj_kernel(x_ref, g_ref, wq_ref, wkv_ref, wf_ref, bf_ref, qg_ref, kg_ref,
                     q_ref, k_ref, v_ref, kb_ref, vb_ref, lf_ref, *, nkv):
    a = (_rms(x_ref[...]) * g_ref[...]).astype(BF16)
    qg = qg_ref[...]
    kg = kg_ref[...]
    for c in range(wq_ref.shape[1] // MXU_N):
        z = _dot(a, wq_ref[:, c * MXU_N:(c + 1) * MXU_N])
        for s in range(MXU_N // LANES):
            lo = c * MXU_N + s * LANES
            q_ref[:, lo:lo + LANES] = (_rms(z[:, s * LANES:(s + 1) * LANES]) * qg).astype(BF16)
    for c in range(nkv // MXU_N):
        z = _dot(a, wkv_ref[:, c * MXU_N:(c + 1) * MXU_N])
        for s in range(MXU_N // LANES):
            lo = c * MXU_N + s * LANES
            kn = _rms(z[:, s * LANES:(s + 1) * LANES]) * kg
            k_ref[:, lo:lo + LANES] = kn
            kb_ref[:, lo:lo + LANES] = kn.astype(BF16)
    for c in range(nkv // MXU_N):
        z = _dot(a, wkv_ref[:, nkv + c * MXU_N:nkv + (c + 1) * MXU_N])
        v_ref[:, c * MXU_N:(c + 1) * MXU_N] = z
        vb_ref[:, c * MXU_N:(c + 1) * MXU_N] = z.astype(BF16)
    zf = _dot(a, wf_ref[...]) + bf_ref[...]
    lf_ref[...] = -(jnp.maximum(-zf, 0.0) + jnp.log1p(jnp.exp(-jnp.abs(zf))))


def _fox_proj(x, g, wq, wkv, wf, bf, qg, kg):
    m, d = x.shape
    nq = wq.shape[1]
    nkv = wkv.shape[1] // 2
    tm = min(ROW_TILE, m)
    row = lambda n: pl.BlockSpec((tm, n), lambda i: (i, 0))
    return pl.pallas_call(
        functools.partial(_fox_proj_kernel, nkv=nkv),
        grid=(m // tm,),
        in_specs=[row(d), _const_spec(g.shape), _const_spec(wq.shape), _const_spec(wkv.shape),
                  _const_spec(wf.shape), _const_spec(bf.shape), _const_spec(qg.shape),
                  _const_spec(kg.shape)],
        out_specs=[row(nq), row(nkv), row(nkv), row(nkv), row(nkv), row(LANES)],
        out_shape=[jax.ShapeDtypeStruct((m, nq), BF16), jax.ShapeDtypeStruct((m, nkv), F32),
                   jax.ShapeDtypeStruct((m, nkv), F32), jax.ShapeDtypeStruct((m, nkv), BF16),
                   jax.ShapeDtypeStruct((m, nkv), BF16), jax.ShapeDtypeStruct((m, LANES), F32)],
        compiler_params=_cparams("arbitrary"),
        name="fox_proj",
    )(x, g, wq, wkv, wf, bf, qg, kg)


def _scan_kernel(x_ref, tri_ref, o_ref, *, suffix):
    tr, n = x_ref.shape
    nblk = n // SCAN_BLOCK
    tri = tri_ref[...]
    carry = jnp.zeros((tr, 1), F32)
    order = range(nblk - 1, -1, -1) if suffix else range(nblk)
    for blk in order:
        cols = slice(blk * SCAN_BLOCK, (blk + 1) * SCAN_BLOCK)
        x = x_ref[:, cols]
        r = _dot(jnp.concatenate(_split3(x), axis=0), tri)
        y = r[0:tr] + r[tr:2 * tr] + r[2 * tr:3 * tr] + carry
        if suffix:
            o_ref[:, cols] = y
            carry = y[:, 0:1] + x[:, 0:1]
        else:
            o_ref[:, cols] = -y
            carry = y[:, SCAN_BLOCK - 1:SCAN_BLOCK]


def _scan_lanes(x, suffix):
    rows, n = x.shape
    tr = min(SCAN_ROWS, rows)
    i = lax.broadcasted_iota(jnp.int32, (SCAN_BLOCK, SCAN_BLOCK), 0)
    j = lax.broadcasted_iota(jnp.int32, (SCAN_BLOCK, SCAN_BLOCK), 1)
    tri = jnp.where(i > j if suffix else i <= j, 1.0, 0.0).astype(BF16)
    blk = pl.BlockSpec((tr, n), lambda i: (i, 0))
    return pl.pallas_call(
        functools.partial(_scan_kernel, suffix=suffix),
        grid=(rows // tr,),
        in_specs=[blk, _const_spec(tri.shape)],
        out_specs=blk,
        out_shape=jax.ShapeDtypeStruct((rows, n), F32),
        compiler_params=_cparams("arbitrary"),
        name="logf_suffix" if suffix else "logf_prefix",
    )(x, tri)


def _fox_prompt_kernel(q_ref, k_ref, v_ref, b_ref, o_ref, qs_ref, m_ref, l_ref, acc_ref, *, group, hd):
    i = pl.program_id(2)
    t = q_ref.shape[0]
    rows = group * t
    for g in range(group):
        qs_ref[g * t:(g + 1) * t, :] = q_ref[:, g * hd:(g + 1) * hd]
    m_ref[...] = jnp.full(m_ref.shape, NEG, F32)
    l_ref[...] = jnp.zeros(l_ref.shape, F32)
    acc_ref[...] = jnp.zeros(acc_ref.shape, F32)

    def tile(kt, masked):
        k0 = pl.multiple_of(kt * t, t)
        s = _dot_nt(qs_ref[...], k_ref[pl.ds(k0, t), :]).reshape(group, t, t)
        s = s + b_ref[0, 0, :, pl.ds(k0, t)][:, None, :]
        if masked:
            r = lax.broadcasted_iota(jnp.int32, (group, t, t), 1)
            c = lax.broadcasted_iota(jnp.int32, (group, t, t), 2)
            s = jnp.where(c <= r, s, NEG)
        s = s.reshape(rows, t)
        m_old = m_ref[...]
        m_new = jnp.maximum(m_old, jnp.max(s, axis=-1, keepdims=True))
        p = jnp.exp(s - jnp.concatenate([m_new] * (t // LANES), axis=1))
        alpha = jnp.exp(m_old - m_new)
        l_ref[...] = alpha * l_ref[...] + jnp.sum(p, axis=-1, keepdims=True)
        acc_ref[...] = alpha * acc_ref[...] + _dot(p.astype(BF16), v_ref[pl.ds(k0, t), :])
        m_ref[...] = m_new

    def body(kt, carry):
        tile(kt, False)
        return carry

    lax.fori_loop(0, i, body, 0)
    tile(i, True)
    o = acc_ref[...] / l_ref[...]
    for g in range(group):
        o_ref[:, g * hd:(g + 1) * hd] = o[g * t:(g + 1) * t].astype(o_ref.dtype)


def _fox_prompt_attn(q, kb, vb, bias, batch, seq, nkv, hd):
    assert hd == LANES
    group = q.shape[1] // (nkv * hd)
    t = FOX_TQ
    nq = seq // t
    return pl.pallas_call(
        functools.partial(_fox_prompt_kernel, group=group, hd=hd),
        grid=(batch, nkv, nq),
        in_specs=[pl.BlockSpec((t, group * hd), lambda b, h, i: (b * nq + i, h)),
                  pl.BlockSpec((seq, hd), lambda b, h, i: (b, h)),
                  pl.BlockSpec((seq, hd), lambda b, h, i: (b, h)),
                  pl.BlockSpec((1, 1, group, seq), lambda b, h, i: (b, h, 0, 0))],
        out_specs=pl.BlockSpec((t, group * hd), lambda b, h, i: (b * nq + i, h)),
        out_shape=jax.ShapeDtypeStruct(q.shape, BF16),
        scratch_shapes=[pltpu.VMEM((group * t, hd), BF16), pltpu.VMEM((group * t, LANES), F32),
                        pltpu.VMEM((group * t, LANES), F32), pltpu.VMEM((group * t, hd), F32)],
        compiler_params=_cparams("arbitrary", "arbitrary", "arbitrary"),
        name="fox_prompt_attn",
    )(q, kb, vb, bias)


def _fox_sample_kernel(pt_ref, q_ref, kn_ref, vn_ref, bn_ref, e_ref, tri_ref, kpool, vpool, lpool,
                       o_ref, kbuf, vbuf, lbuf, sem, *, nkv, hd, n_chunks, rows_q, group):
    b = pl.program_id(0)
    nb = pl.num_programs(0)
    page_rows = PAGE_SIZE * nkv
    lc = PAGES_PER_CHUNK * PAGE_SIZE
    nh = lbuf.shape[2]
    pages_per_block = SCAN_BLOCK // PAGE_SIZE
    nblk = lc // SCAN_BLOCK

    def start_chunk(bb, step, slot):
        c = n_chunks - 1 - step
        for p in range(PAGES_PER_CHUNK):
            page = pt_ref[bb, c * PAGES_PER_CHUNK + p]
            src = pl.ds(pl.multiple_of(page * page_rows, page_rows), page_rows)
            dst = pl.ds(p * page_rows, page_rows)
            pltpu.make_async_copy(kpool.at[src, :], kbuf.at[slot, dst, :], sem.at[0, slot]).start()
            pltpu.make_async_copy(vpool.at[src, :], vbuf.at[slot, dst, :], sem.at[1, slot]).start()
            pltpu.make_async_copy(lpool.at[page], lbuf.at[slot, p], sem.at[2, slot]).start()

    def wait_chunk(slot):
        for buf, s in ((kbuf, 0), (vbuf, 1), (lbuf, 2)):
            pltpu.make_async_copy(buf.at[slot], buf.at[slot], sem.at[s, slot]).wait()

    @pl.when(b == 0)
    def _():
        start_chunk(0, 0, 0)

    e = e_ref[...]
    tri = tri_ref[...]
    q = [q_ref[0, h] for h in range(nkv)]

    def chunk(step, carry):
        slot = lax.rem(b * n_chunks + step, 2)
        nxt_step = lax.rem(step + 1, n_chunks)
        nxt_b = jnp.where(step + 1 == n_chunks, b + 1, b)

        @pl.when(nxt_b < nb)
        def _():
            start_chunk(nxt_b, nxt_step, 1 - slot)

        wait_chunk(slot)

        xs = [jnp.concatenate([lbuf[slot, k * pages_per_block + p] for p in range(pages_per_block)], axis=1)
              for k in range(nblk)]
        r = _dot(jnp.concatenate([part for x in xs for part in _split3(x)], axis=0), tri)
        sfx_carry = carry[nkv]
        ys = [None] * nblk
        for k in range(nblk - 1, -1, -1):
            local = r[(3 * k) * nh:(3 * k + 1) * nh] + r[(3 * k + 1) * nh:(3 * k + 2) * nh] \
                + r[(3 * k + 2) * nh:(3 * k + 3) * nh]
            ys[k] = local + sfx_carry
            sfx_carry = sfx_carry + (local[:, 0:1] + xs[k][:, 0:1])
        bias = _dot(e, jnp.concatenate(_split3(jnp.concatenate(ys, axis=1)), axis=0))
        new = []
        for h in range(nkv):
            m_old, l_old, acc_old = carry[h]
            k = kbuf[slot, pl.ds(h, lc, stride=nkv), :].astype(BF16)
            v = vbuf[slot, pl.ds(h, lc, stride=nkv), :].astype(BF16)
            s = _dot_nt(q[h], k) + bias[h * rows_q:(h + 1) * rows_q]
            m_new = jnp.maximum(m_old, jnp.max(s, axis=-1, keepdims=True))
            p = jnp.exp(s - m_new)
            alpha = jnp.exp(m_old - m_new)
            new.append((m_new, alpha * l_old + jnp.sum(p, axis=-1, keepdims=True),
                        alpha * acc_old + _dot(p.astype(BF16), v)))
        return tuple(new) + (sfx_carry,)

    init = tuple((jnp.full((rows_q, 1), NEG, F32), jnp.zeros((rows_q, 1), F32),
                  jnp.zeros((rows_q, hd), F32)) for _ in range(nkv)) + (jnp.zeros((nh, 1), F32),)
    state = lax.fori_loop(0, n_chunks, chunk, init)

    tn = kn_ref.shape[1]
    bias_n = _dot_nt(e, bn_ref[0])
    r = lax.broadcasted_iota(jnp.int32, (rows_q, tn), 0)
    c = lax.broadcasted_iota(jnp.int32, (rows_q, tn), 1)
    ok = c * group <= r
    for h in range(nkv):
        m_old, l_old, acc_old = state[h]
        k = kn_ref[0, :, h * hd:(h + 1) * hd].astype(BF16)
        v = vn_ref[0, :, h * hd:(h + 1) * hd].astype(BF16)
        s = jnp.where(ok, _dot_nt(q[h], k) + bias_n[h * rows_q:(h + 1) * rows_q], NEG)
        m_new = jnp.maximum(m_old, jnp.max(s, axis=-1, keepdims=True))
        p = jnp.exp(s - m_new)
        alpha = jnp.exp(m_old - m_new)
        l_new = alpha * l_old + jnp.sum(p, axis=-1, keepdims=True)
        acc = alpha * acc_old + _dot(p.astype(BF16), v)
        o_ref[0, h] = (acc / l_new).astype(o_ref.dtype)


def _fox_sample_attn(page_table, q, k_new, v_new, bias_new3, emat, kpool, vpool, lpool, nkv, hd, group):
    nb, _, rows_q, _ = q.shape
    n_pages = page_table.shape[1]
    nh = lpool.shape[1]
    n_chunks = n_pages // PAGES_PER_CHUNK
    buf_rows = PAGES_PER_CHUNK * PAGE_SIZE * nkv
    tn = k_new.shape[1]
    i = lax.broadcasted_iota(jnp.int32, (SCAN_BLOCK, SCAN_BLOCK), 0)
    j = lax.broadcasted_iota(jnp.int32, (SCAN_BLOCK, SCAN_BLOCK), 1)
    tri = jnp.where(i > j, 1.0, 0.0).astype(BF16)
    const = lambda a: pl.BlockSpec(a.shape, lambda b, pt: (0,) * a.ndim)
    grid_spec = pltpu.PrefetchScalarGridSpec(
        num_scalar_prefetch=1,
        grid=(nb,),
        in_specs=[pl.BlockSpec((1, nkv, rows_q, hd), lambda b, pt: (b, 0, 0, 0)),
                  pl.BlockSpec((1, tn, nkv * hd), lambda b, pt: (b, 0, 0)),
                  pl.BlockSpec((1, tn, nkv * hd), lambda b, pt: (b, 0, 0)),
                  pl.BlockSpec((1,) + bias_new3.shape[1:], lambda b, pt: (b, 0, 0)),
                  const(emat), const(tri),
                  pl.BlockSpec(memory_space=pl.ANY),
                  pl.BlockSpec(memory_space=pl.ANY),
                  pl.BlockSpec(memory_space=pl.ANY)],
        out_specs=pl.BlockSpec((1, nkv, rows_q, hd), lambda b, pt: (b, 0, 0, 0)),
        scratch_shapes=[pltpu.VMEM((2, buf_rows, hd), F32), pltpu.VMEM((2, buf_rows, hd), F32),
                        pltpu.VMEM((2, PAGES_PER_CHUNK, nh, PAGE_SIZE), F32),
                        pltpu.SemaphoreType.DMA((3, 2))],
    )
    return pl.pallas_call(
        functools.partial(_fox_sample_kernel, nkv=nkv, hd=hd, n_chunks=n_chunks, rows_q=rows_q, group=group),
        grid_spec=grid_spec,
        out_shape=jax.ShapeDtypeStruct(q.shape, BF16),
        compiler_params=_cparams("arbitrary"),
        name="fox_sample_attn",
    )(page_table, q, k_new, v_new, bias_new3, emat, tri, kpool, vpool, lpool)


def _out_proj_kernel(o_ref, h_ref, w_ref, g_ref, h1_ref, c_ref):
    h1 = h_ref[...] + _dot(o_ref[...], w_ref[...])
    h1_ref[...] = h1
    c_ref[...] = (_rms(h1) * g_ref[...]).astype(BF16)


def _out_proj(o, h, w, g):
    m, d = h.shape
    tm = min(ROW_TILE, m)
    row = lambda n: pl.BlockSpec((tm, n), lambda i: (i, 0))
    return pl.pallas_call(
        _out_proj_kernel,
        grid=(m // tm,),
        in_specs=[row(o.shape[1]), row(d), _const_spec(w.shape), _const_spec(g.shape)],
        out_specs=[row(d), row(d)],
        out_shape=[jax.ShapeDtypeStruct((m, d), F32), jax.ShapeDtypeStruct((m, d), BF16)],
        compiler_params=_cparams("arbitrary"),
        name="attn_out_proj",
    )(o, h, w, g)


def _conv_gate(ug, uv, cwg_ref, cwv_ref, cbg_ref, cbv_ref, taps):
    def conv(u_taps, cw_ref, cb_ref):
        y = cb_ref[...]
        for j in range(CONV_W):
            y = y + cw_ref[j:j + 1, :] * u_taps[j]
        return y
    g = conv(taps(ug), cwg_ref, cbg_ref)
    v = conv(taps(uv), cwv_ref, cbv_ref)
    return (jax.nn.gelu(g, approximate=True) * v).astype(BF16)


def _ffn_epilogue(j, h_ref, gn_ref, acc_ref, h2_ref, n_ref):
    @pl.when(j == pl.num_programs(1) - 1)
    def _():
        h2 = h_ref[...] + acc_ref[...]
        h2_ref[...] = h2
        n_ref[...] = (_rms(h2) * gn_ref[...]).astype(BF16)


def _ffn_prompt_kernel(c_ref, halo_ref, wg_ref, wv_ref, cwg_ref, cwv_ref, cbg_ref, cbv_ref, wo_ref,
                       h_ref, gn_ref, h2_ref, n_ref, acc_ref, *, tiles_per_seq):
    i, j = pl.program_id(0), pl.program_id(1)
    tm = c_ref.shape[0]

    @pl.when(j == 0)
    def _():
        acc_ref[...] = jnp.zeros(acc_ref.shape, F32)

    halo = halo_ref[...]
    halo = jnp.where(lax.rem(i, tiles_per_seq) == 0, jnp.zeros_like(halo), halo)
    x = jnp.concatenate([halo, c_ref[...]], axis=0)
    taps = lambda u: [u[HALO - (CONV_W - 1) + k:HALO - (CONV_W - 1) + k + tm] for k in range(CONV_W)]
    act = _conv_gate(_dot(x, wg_ref[...]), _dot(x, wv_ref[...]), cwg_ref, cwv_ref, cbg_ref, cbv_ref, taps)
    acc_ref[...] += _dot(act, wo_ref[...])
    _ffn_epilogue(j, h_ref, gn_ref, acc_ref, h2_ref, n_ref)


def _ffn_sample_kernel(c_ref, pg_ref, pv_ref, wg_ref, wv_ref, cwg_ref, cwv_ref, cbg_ref, cbv_ref, wo_ref,
                       h_ref, gn_ref, h2_ref, n_ref, acc_ref):
    j = pl.program_id(1)
    nb = pg_ref.shape[1]
    nt = c_ref.shape[0] // nb

    @pl.when(j == 0)
    def _():
        acc_ref[...] = jnp.zeros(acc_ref.shape, F32)

    x = c_ref[...]

    def taps_of(p_ref):
        def taps(u):
            slabs = [p_ref[k] for k in range(CONV_W - 1)] + [u[t * nb:(t + 1) * nb] for t in range(nt)]
            return [jnp.concatenate(slabs[k:k + nt], axis=0) for k in range(CONV_W)]
        return taps

    def conv(u, p_ref, cw_ref, cb_ref):
        y = cb_ref[...]
        for k, tap in enumerate(taps_of(p_ref)(u)):
            y = y + cw_ref[k:k + 1, :] * tap
        return y

    g = conv(_dot(x, wg_ref[...]), pg_ref, cwg_ref, cbg_ref)
    v = conv(_dot(x, wv_ref[...]), pv_ref, cwv_ref, cbv_ref)
    act = (jax.nn.gelu(g, approximate=True) * v).astype(BF16)
    acc_ref[...] += _dot(act, wo_ref[...])
    _ffn_epilogue(j, h_ref, gn_ref, acc_ref, h2_ref, n_ref)


def _conv_ffn(c, h, layer, w_in, conv_w, conv_b, w_out, g_next, past, tiles_per_seq):
    m, d = h.shape
    dff = w_out.shape[1]
    tf = FFN_TILE
    nf = dff // tf
    tm = min(ROW_TILE, m)
    wspecs = [pl.BlockSpec((None, d, tf), lambda i, j: (layer, 0, j)),
              pl.BlockSpec((None, d, tf), lambda i, j: (layer, 0, nf + j)),
              pl.BlockSpec((None, CONV_W, tf), lambda i, j: (layer, 0, j)),
              pl.BlockSpec((None, CONV_W, tf), lambda i, j: (layer, 0, nf + j)),
              pl.BlockSpec((None, 1, tf), lambda i, j: (layer, 0, j)),
              pl.BlockSpec((None, 1, tf), lambda i, j: (layer, 0, nf + j)),
              pl.BlockSpec((None, tf, d), lambda i, j: (layer, j, 0))]
    wargs = [w_in, w_in, conv_w, conv_w, conv_b, conv_b, w_out]
    row = pl.BlockSpec((tm, d), lambda i, j: (i, 0))
    tail_specs = [row, pl.BlockSpec((1, d), lambda i, j: (0, 0))]
    if past is None:
        per = tm // HALO
        head_specs = [row, pl.BlockSpec((HALO, d), lambda i, j: (jnp.maximum(i * per - 1, 0), 0))]
        head_args = [c, c]
        kern = functools.partial(_ffn_prompt_kernel, tiles_per_seq=tiles_per_seq)
    else:
        nb = past.shape[2]
        head_specs = [row, pl.BlockSpec((None, CONV_W - 1, nb, tf), lambda i, j: (layer, 0, 0, j)),
                      pl.BlockSpec((None, CONV_W - 1, nb, tf), lambda i, j: (layer, 0, 0, nf + j))]
        head_args = [c, past, past]
        kern = _ffn_sample_kernel
    return pl.pallas_call(
        kern,
        grid=(m // tm, nf),
        in_specs=head_specs + wspecs + tail_specs,
        out_specs=[row, row],
        out_shape=[jax.ShapeDtypeStruct((m, d), F32), jax.ShapeDtypeStruct((m, d), BF16)],
        scratch_shapes=[pltpu.VMEM((tm, d), F32)],
        compiler_params=_cparams("arbitrary", "arbitrary"),
        name="conv_ffn_prompt" if past is None else "conv_ffn_sample",
    )(*head_args, *wargs, h, g_next)


def _matmul_kernel(x_ref, w_ref, o_ref):
    o_ref[...] = _dot(x_ref[...], w_ref[...])


def _matmul(x, w, layer, tn=1024):
    m, k = x.shape
    n = w.shape[2]
    return pl.pallas_call(
        _matmul_kernel,
        grid=(n // tn,),
        in_specs=[pl.BlockSpec((m, k), lambda j: (0, 0)), pl.BlockSpec((None, k, tn), lambda j: (layer, 0, j))],
        out_specs=pl.BlockSpec((m, tn), lambda j: (0, j)),
        out_shape=jax.ShapeDtypeStruct((m, n), F32),
        compiler_params=_cparams("arbitrary"),
        name="conv_state_rows",
    )(x, w)


def _ple_kernel(h_ref, n_ref, p_ref, wp_ref, wg_ref, *rest, n_norms):
    g_refs = rest[:n_norms]
    h3_ref = rest[n_norms]
    a_refs = rest[n_norms + 1:]
    n = n_ref[...]
    p = p_ref[...].astype(BF16)
    d = h_ref.shape[1]
    tn = 512
    for c in range(d // tn):
        cs = slice(c * tn, (c + 1) * tn)
        gate = jax.nn.sigmoid(_dot(n, wg_ref[:, cs]))
        h3_ref[:, cs] = h_ref[:, cs] + _dot(p, wp_ref[:, cs]) * gate
    if n_norms:
        xhat = _rms(h3_ref[...])
        for g_ref, a_ref in zip(g_refs, a_refs):
            a_ref[...] = (xhat * g_ref[...]).astype(BF16)


def _ple(h, n, p, layer, wp, wg, gains):
    m, d = h.shape
    tm = min(ROW_TILE, m)
    row = lambda w: pl.BlockSpec((tm, w), lambda i: (i, 0))
    k = len(gains)
    outs = pl.pallas_call(
        functools.partial(_ple_kernel, n_norms=k),
        grid=(m // tm,),
        in_specs=[row(d), row(d), pl.BlockSpec((None, tm, p.shape[2]), lambda i: (layer, i, 0)),
                  pl.BlockSpec((None,) + wp.shape[1:], lambda i: (layer, 0, 0)),
                  pl.BlockSpec((None,) + wg.shape[1:], lambda i: (layer, 0, 0))]
                 + [_const_spec(g.shape) for g in gains],
        out_specs=[row(d)] * (k + 1),
        out_shape=[jax.ShapeDtypeStruct((m, d), F32)] + [jax.ShapeDtypeStruct((m, d), BF16)] * k,
        compiler_params=_cparams("arbitrary"),
        name="ple",
    )(h, n, p, wp, wg, *gains)
    return outs


def _swa_proj_kernel(a_ref, akv_ref, wq_ref, wkv_ref, qg_ref, kg_ref, q_ref, k_ref, v_ref, *, nkv, hd):
    a = a_ref[...]
    qg = qg_ref[...]
    lane = lax.broadcasted_iota(jnp.int32, (a.shape[0], LANES), 1)
    low = lane < hd
    for c in range(wq_ref.shape[1] // MXU_N):
        z = _dot(a, wq_ref[:, c * MXU_N:(c + 1) * MXU_N])
        for s in range(MXU_N // LANES):
            zz = z[:, s * LANES:(s + 1) * LANES]
            sq = zz * zz
            s_lo = jnp.sum(jnp.where(low, sq, 0.0), axis=-1, keepdims=True)
            s_hi = jnp.sum(jnp.where(low, 0.0, sq), axis=-1, keepdims=True)
            inv = jnp.where(low, lax.rsqrt(s_lo / hd + EPS), lax.rsqrt(s_hi / hd + EPS))
            lo = c * MXU_N + s * LANES
            q_ref[:, lo:lo + LANES] = (zz * inv * qg).astype(BF16)
    akv = akv_ref[...]
    kg = kg_ref[...]
    for c in range(nkv // MXU_N):
        z = _dot(akv, wkv_ref[:, c * MXU_N:(c + 1) * MXU_N])
        for s in range(MXU_N // LANES):
            lo = c * MXU_N + s * LANES
            k_ref[:, lo:lo + LANES] = _rms(z[:, s * LANES:(s + 1) * LANES]) * kg
    for c in range(nkv // MXU_N):
        v_ref[:, c * MXU_N:(c + 1) * MXU_N] = _dot(akv, wkv_ref[:, nkv + c * MXU_N:nkv + (c + 1) * MXU_N])


def _swa_proj(a, akv, wq, wkv_dup, qg, kg, hd):
    m, d = a.shape
    nkv = wkv_dup.shape[1] // 2
    tm = min(ROW_TILE, m)
    row = lambda n: pl.BlockSpec((tm, n), lambda i: (i, 0))
    return pl.pallas_call(
        functools.partial(_swa_proj_kernel, nkv=nkv, hd=hd),
        grid=(m // tm,),
        in_specs=[row(d), row(d), _const_spec(wq.shape), _const_spec(wkv_dup.shape),
                  _const_spec(qg.shape), _const_spec(kg.shape)],
        out_specs=[row(wq.shape[1]), row(nkv), row(nkv)],
        out_shape=[jax.ShapeDtypeStruct((m, wq.shape[1]), BF16), jax.ShapeDtypeStruct((m, nkv), F32),
                   jax.ShapeDtypeStruct((m, nkv), F32)],
        compiler_params=_cparams("arbitrary"),
        name="swa_proj",
    )(a, akv, wq, wkv_dup, qg, kg)


def _swa_prompt_kernel(slope_ref, sink_ref, q_ref, kp_ref, kc_ref, vp_ref, vc_ref, o_ref, *, nkv, hd):
    i = pl.program_id(1)
    w = q_ref.shape[0]
    pairs = q_ref.shape[1] // (nkv * LANES)
    lane = lax.broadcasted_iota(jnp.int32, (w, LANES), 1)
    low = lane < hd
    r = lax.broadcasted_iota(jnp.int32, (2 * w, 2 * w), 0)
    c = lax.broadcasted_iota(jnp.int32, (2 * w, 2 * w), 1)
    qpos = jnp.where(r < w, r, r - w) + w
    dist = qpos - c
    valid = (dist >= 0) & (dist < w) & ((c >= w) | (i > 0))
    distf = dist.astype(F32)
    top = r < w
    rcol = lax.broadcasted_iota(jnp.int32, (2 * w, 1), 0) < w
    for h in range(nkv):
        k = jnp.concatenate([kp_ref[:, h * LANES:(h + 1) * LANES], kc_ref[:, h * LANES:(h + 1) * LANES]],
                            axis=0).astype(BF16)
        v = jnp.concatenate([vp_ref[:, h * LANES:(h + 1) * LANES], vc_ref[:, h * LANES:(h + 1) * LANES]],
                            axis=0).astype(BF16)
        for pr in range(pairs):
            col = (h * pairs + pr) * LANES
            h0 = 2 * (h * pairs + pr)
            qp = q_ref[:, col:col + LANES]
            rows = jnp.concatenate([jnp.where(low, qp, jnp.zeros_like(qp)),
                                    jnp.where(low, jnp.zeros_like(qp), qp)], axis=0)
            slope = jnp.where(top, slope_ref[h0], slope_ref[h0 + 1])
            s = jnp.where(valid, _dot_nt(rows, k) - slope * distf, NEG)
            sink = jnp.where(rcol, sink_ref[h0], sink_ref[h0 + 1])
            m = jnp.maximum(jnp.max(s, axis=-1, keepdims=True), sink)
            p = jnp.exp(s - m)
            den = jnp.sum(p, axis=-1, keepdims=True) + jnp.exp(sink - m)
            o = _dot(p.astype(BF16), v) / den
            o_ref[:, col:col + LANES] = jnp.where(low, o[:w], o[w:]).astype(o_ref.dtype)


def _swa_prompt_attn(q, kdup, vdup, slopes, sinks, batch, seq, nkv, hd):
    nblk = seq // WINDOW
    dq = q.shape[1]
    dk = kdup.shape[1]
    cur = lambda b, i: (b * nblk + i, 0)
    prev = lambda b, i: (b * nblk + jnp.maximum(i - 1, 0), 0)
    smem = pl.BlockSpec(memory_space=pltpu.SMEM)
    return pl.pallas_call(
        functools.partial(_swa_prompt_kernel, nkv=nkv, hd=hd),
        grid=(batch, nblk),
        in_specs=[smem, smem, pl.BlockSpec((WINDOW, dq), cur), pl.BlockSpec((WINDOW, dk), prev),
                  pl.BlockSpec((WINDOW, dk), cur), pl.BlockSpec((WINDOW, dk), prev),
                  pl.BlockSpec((WINDOW, dk), cur)],
        out_specs=pl.BlockSpec((WINDOW, dq), cur),
        out_shape=jax.ShapeDtypeStruct(q.shape, BF16),
        compiler_params=_cparams("arbitrary", "arbitrary"),
        name="swa_prompt_attn",
    )(slopes, sinks, q, kdup, kdup, vdup, vdup)


def _swa_sample_kernel(q_ref, kc_ref, vc_ref, kn_ref, vn_ref, slope_ref, sink_ref, o_ref, *, heads):
    q = q_ref[0]
    rows = q.shape[0]
    w = kc_ref.shape[1]
    tn = kn_ref.shape[1]
    slope = slope_ref[...]
    sink = sink_ref[...]
    t_p = lax.broadcasted_iota(jnp.int32, (rows, w), 0) // heads
    j_p = lax.broadcasted_iota(jnp.int32, (rows, w), 1)
    dist_p = t_p + w - j_p
    s_p = jnp.where(dist_p < WINDOW,_dot_nt(q, kc_ref[0].astype(BF16)) - slope * dist_p.astype(F32), NEG)
    t_n = lax.broadcasted_iota(jnp.int32, (rows, tn), 0) // heads
    j_n = lax.broadcasted_iota(jnp.int32, (rows, tn), 1)
    dist_n = t_n - j_n
    s_n = jnp.where(dist_n >= 0, _dot_nt(q, kn_ref[0].astype(BF16)) - slope * dist_n.astype(F32), NEG)
    m = jnp.maximum(jnp.maximum(jnp.max(s_p, axis=-1, keepdims=True), jnp.max(s_n, axis=-1, keepdims=True)), sink)
    p_p = jnp.exp(s_p - m)
    p_n = jnp.exp(s_n - m)
    den = jnp.sum(p_p, axis=-1, keepdims=True) + jnp.sum(p_n, axis=-1, keepdims=True) + jnp.exp(sink - m)
    o = _dot(p_p.astype(BF16), vc_ref[0].astype(BF16)) + _dot(p_n.astype(BF16), vn_ref[0].astype(BF16))
    o_ref[0] = o / den


def _swa_sample_attn(q, kc, vc, kn, vn, slope_col, sink_col, heads):
    nb, rows, dk = q.shape
    blk = lambda a: pl.BlockSpec((1,) + a.shape[1:], lambda b: (b, 0, 0))
    return pl.pallas_call(
        functools.partial(_swa_sample_kernel, heads=heads),
        grid=(nb,),
        in_specs=[blk(q), blk(kc), blk(vc), blk(kn), blk(vn), _const_spec(slope_col.shape),
                  _const_spec(sink_col.shape)],
        out_specs=blk(q),
        out_shape=jax.ShapeDtypeStruct((nb, rows, dk), F32),
        compiler_params=_cparams("arbitrary"),
        name="swa_sample_attn",
    )(q, kc, vc, kn, vn, slope_col, sink_col)


def kernel(x_prompt, x_sample, cache_fox_k, cache_fox_v, cache_fox_logf, cache_win_k, cache_win_v, state_conv, page_table, p_prompt, p_sample, norm_attn_g, norm_ffn_g, norm_ple_g, fox_w_in, fox_b_f, fox_q_norm_g, fox_k_norm_g, fox_w_out, kv_norm_g, swa_w_kv, swa_k_norm_g, swa_w_q, swa_q_norm_g, swa_sinks, swa_w_out, ffn_w_in, ffn_conv_w, ffn_conv_b, ffn_w_out, ple_w_proj, ple_w_gate):
    B, S, D = x_prompt.shape
    NB, T, _ = x_sample.shape
    HD_A = fox_q_norm_g.shape[-1]
    H_A = fox_b_f.shape[-1]
    KVH_A = cache_fox_k.shape[3]
    G_A = H_A // KVH_A
    HD_B = swa_q_norm_g.shape[-1]
    H_B = swa_sinks.shape[-1]
    KVH_B = cache_win_k.shape[2]
    G_B = H_B // KVH_B
    WIN = cache_win_k.shape[1]
    L = page_table.shape[1] * PAGE_SIZE
    DFF = ffn_w_out.shape[1]
    NQ_A, NK_A = H_A * HD_A, KVH_A * HD_A
    row2 = lambda g: g.reshape(1, -1)

    w_in0 = fox_w_in[0]
    wq_a = w_in0[:, :NQ_A].astype(BF16)
    wkv_a = w_in0[:, NQ_A:NQ_A + 2 * NK_A].astype(BF16)
    wf_a = jnp.pad(w_in0[:, NQ_A + 2 * NK_A:], ((0, 0), (0, LANES - H_A))).astype(BF16)
    bf_a = jnp.pad(fox_b_f[0], (0, LANES - H_A)).reshape(1, LANES)
    qg_a = row2(fox_q_norm_g[0] * (HD_A ** -0.5))
    kg_a = row2(fox_k_norm_g[0])
    wo_a = fox_w_out[0].astype(BF16)
    wq_b = swa_w_q[0].astype(BF16)
    dup = lambda w: jnp.repeat(w.reshape(D, KVH_B, 1, HD_B), LANES // HD_B, axis=2).reshape(D, KVH_B * LANES)
    wk_b, wv_b = jnp.split(swa_w_kv, 2, axis=-1)
    wkv_b = jnp.concatenate([dup(wk_b), dup(wv_b)], axis=1).astype(BF16)
    qg_b = row2(jnp.tile(swa_q_norm_g[0] * (HD_B ** -0.5), LANES // HD_B))
    kg_b = row2(jnp.tile(swa_k_norm_g, LANES // HD_B))
    wo_b = swa_w_out[0].astype(BF16)
    ffn_wi = ffn_w_in.astype(BF16)
    ffn_wo = ffn_w_out.astype(BF16)
    ple_wp = ple_w_proj.astype(BF16)
    ple_wg = ple_w_gate.astype(BF16)
    slopes = jnp.exp2(-8.0 * jnp.arange(1, H_B + 1, dtype=F32) / H_B)
    sinks = swa_sinks[0]

    xp = x_prompt.reshape(B * S, D)
    xs = x_sample.transpose(1, 0, 2).reshape(T * NB, D)
    pp = p_prompt.reshape(2, B * S, -1)
    ps = p_sample.transpose(0, 2, 1, 3).reshape(2, T * NB, -1)
    past = state_conv.transpose(0, 2, 1, 3)

    g0 = row2(norm_attn_g[0])
    q_p, k_p, v_p, kb_p, vb_p, lf_p = _fox_proj(xp, g0, wq_a, wkv_a, wf_a, bf_a, qg_a, kg_a)
    q_s, k_s, v_s, _, _, lf_s = _fox_proj(xs, g0, wq_a, wkv_a, wf_a, bf_a, qg_a, kg_a)
    lf_p = lf_p[:, :H_A]
    lf_s = lf_s[:, :H_A]

    bias_p = _scan_lanes(lf_p.reshape(B, S, H_A).transpose(0, 2, 1).reshape(B * H_A, S), suffix=False)
    o_p = _fox_prompt_attn(q_p, kb_p, vb_p, bias_p.reshape(B, KVH_A, G_A, S), B, S, KVH_A, HD_A)

    tn = 16
    lf_new = lf_s.reshape(T, NB, H_A).transpose(1, 2, 0).reshape(NB * H_A, T)
    bias_new = _scan_lanes(jnp.pad(lf_new, ((0, 0), (0, SCAN_BLOCK - T))), suffix=False)[:, :tn]
    bias_new3 = jnp.concatenate(_split3(bias_new.reshape(NB, H_A, tn).transpose(0, 2, 1)), axis=-1)
    rows_q = T * G_A
    head_of_row = (jnp.arange(KVH_A)[:, None] * G_A + jnp.arange(rows_q)[None, :] % G_A).reshape(-1)
    emat = jnp.tile(jax.nn.one_hot(head_of_row, H_A, dtype=BF16), (1, 3))
    qs_b = q_s.reshape(T, NB, KVH_A, G_A, HD_A).transpose(1, 2, 0, 3, 4).reshape(NB, KVH_A, rows_q, HD_A)
    pad_new = lambda a: jnp.pad(a.reshape(T, NB, NK_A).transpose(1, 0, 2), ((0, 0), (0, tn - T), (0, 0)))
    kpool = cache_fox_k.reshape(-1, HD_A)
    vpool = cache_fox_v.reshape(-1, HD_A)
    o_s = _fox_sample_attn(page_table, qs_b, pad_new(k_s), pad_new(v_s), bias_new3, emat, kpool, vpool,
                           cache_fox_logf.reshape(cache_fox_logf.shape[1:]).transpose(0, 2, 1), KVH_A, HD_A, G_A)
    o_s = o_s.reshape(NB, KVH_A, T, G_A, HD_A).transpose(2, 0, 1, 3, 4).reshape(T * NB, NQ_A)

    gf0 = row2(norm_ffn_g[0])
    h_p, c_p = _out_proj(o_p, xp, wo_a, gf0)
    h_s, c_s = _out_proj(o_s, xs, wo_a, gf0)

    gp0 = row2(norm_ple_g[0])
    tiles_per_seq = S // ROW_TILE
    conv_b = ffn_conv_b[:, None, :]
    h_p, n_p = _conv_ffn(c_p, h_p, 0, ffn_wi, ffn_conv_w, conv_b, ffn_wo, gp0, None, tiles_per_seq)
    h_s, n_s = _conv_ffn(c_s, h_s, 0, ffn_wi, ffn_conv_w, conv_b, ffn_wo, gp0, past, 1)
    tail_p = lambda c: c.reshape(B, S, D)[:, S - (CONV_W - 1):].reshape(B * (CONV_W - 1), D)
    tail_rows = lambda cp, cs: jnp.concatenate(
        [jnp.pad(tail_p(cp), ((0, HALO - B * (CONV_W - 1)), (0, 0))), cs[(T - (CONV_W - 1)) * NB:]], axis=0)
    u0 = _matmul(tail_rows(c_p, c_s), ffn_wi, 0)

    gq1, gkv = row2(norm_attn_g[1]), row2(kv_norm_g)
    h_p, a_p, akv_p = _ple(h_p, n_p, pp, 0, ple_wp, ple_wg, [gq1, gkv])
    h_s, a_s, akv_s = _ple(h_s, n_s, ps, 0, ple_wp, ple_wg, [gq1, gkv])

    q1_p, kd_p, vd_p = _swa_proj(a_p, akv_p, wq_b, wkv_b, qg_b, kg_b, HD_B)
    q1_s, kd_s, vd_s = _swa_proj(a_s, akv_s, wq_b, wkv_b, qg_b, kg_b, HD_B)
    undup = lambda a: a.reshape(a.shape[0], KVH_B, LANES // HD_B, HD_B)[:, :, 0]
    o1_p = _swa_prompt_attn(q1_p, kd_p, vd_p, slopes, sinks, B, S, KVH_B, HD_B)

    ksh_s = undup(kd_s).reshape(T, NB, KVH_B * HD_B).transpose(1, 0, 2)
    vsh_s = undup(vd_s).reshape(T, NB, KVH_B * HD_B).transpose(1, 0, 2)
    q1h = q1_s.reshape(T, NB, H_B, 1, HD_B).transpose(1, 0, 2, 3, 4)
    slab = (jnp.arange(H_B)[:, None] // G_B == jnp.arange(KVH_B)[None, :])[None, None, :, :, None]
    q1x = jnp.where(slab, q1h, jnp.zeros((), BF16)).reshape(NB, T * H_B, KVH_B * HD_B)
    padn = lambda a: jnp.pad(a, ((0, 0), (0, tn - T), (0, 0)))
    slope_col = jnp.tile(slopes, T).reshape(T * H_B, 1)
    sink_col = jnp.tile(sinks, T).reshape(T * H_B, 1)
    o1x = _swa_sample_attn(q1x, cache_win_k.reshape(NB, WIN, -1), cache_win_v.reshape(NB, WIN, -1),
                           padn(ksh_s), padn(vsh_s), slope_col, sink_col, H_B)
    o1x = o1x.reshape(NB, T, H_B, KVH_B, HD_B)
    o1_s = jnp.sum(jnp.where(slab, o1x, 0.0), axis=3)
    o1_s = o1_s.transpose(1, 0, 2, 3).reshape(T * NB, H_B * HD_B).astype(BF16)

    gf1 = row2(norm_ffn_g[1])
    h_p, c_p = _out_proj(o1_p, h_p, wo_b, gf1)
    h_s, c_s = _out_proj(o1_s, h_s, wo_b, gf1)

    gp1 = row2(norm_ple_g[1])
    h_p, n_p = _conv_ffn(c_p, h_p, 1, ffn_wi, ffn_conv_w, conv_b, ffn_wo, gp1, None, tiles_per_seq)
    h_s, n_s = _conv_ffn(c_s, h_s, 1, ffn_wi, ffn_conv_w, conv_b, ffn_wo, gp1, past, 1)
    u1 = _matmul(tail_rows(c_p, c_s), ffn_wi, 1)

    (y_p,) = _ple(h_p, n_p, pp, 1, ple_wp, ple_wg, [])
    (y_s,) = _ple(h_s, n_s, ps, 1, ple_wp, ple_wg, [])

    to_bt = lambda a, *tail: a.reshape((T, NB) + tail).swapaxes(0, 1)
    y_prompt = y_p.reshape(B, S, D)
    y_sample = to_bt(y_s, D)
    fk_p = k_p.reshape(1, B, S, KVH_A, HD_A)
    fv_p = v_p.reshape(1, B, S, KVH_A, HD_A)
    flf_p = lf_p.reshape(1, B, S, H_A)
    ksh_p = undup(kd_p).reshape(B, S, KVH_B, HD_B)
    vsh_p = undup(vd_p).reshape(B, S, KVH_B, HD_B)
    win_k_p = ksh_p[:, S - WIN:]
    win_v_p = vsh_p[:, S - WIN:]
    nrow_p = B * (CONV_W - 1)
    conv_p = jnp.stack([u[:nrow_p].reshape(B, CONV_W - 1, 2 * DFF) for u in (u0, u1)])
    conv_s = jnp.stack([u[HALO:].reshape(CONV_W - 1, NB, 2 * DFF).swapaxes(0, 1) for u in (u0, u1)])
    fk_s = to_bt(k_s, KVH_A, HD_A)[None]
    fv_s = to_bt(v_s, KVH_A, HD_A)[None]
    flf_s = to_bt(lf_s, H_A)[None]
    win_k_s = jnp.concatenate([cache_win_k, ksh_s.reshape(NB, T, KVH_B, HD_B)], axis=1)[:, -WIN:]
    win_v_s = jnp.concatenate([cache_win_v, vsh_s.reshape(NB, T, KVH_B, HD_B)], axis=1)[:, -WIN:]
    return (y_prompt, y_sample, fk_p, fv_p, flf_p, win_k_p, win_v_p, conv_p,
            fk_s, fv_s, flf_s, win_k_s, win_v_s, conv_s)
```

```python
import functools

import jax
import jax.numpy as jnp
from jax import lax
from jax.experimental import pallas as pl
from jax.experimental.pallas import tpu as pltpu

F32 = jnp.float32
BF16 = jnp.bfloat16

EPS = 1e-6
PAGE_SIZE = 128
WINDOW = 128
CONV_W = 3
NEG = -1e30
MASK_DIST = 1e9
LOG2E = 1.4426950408889634
LANES = 128
MXU_N = 256
V7X_VMEM_LIMIT = 56 * 1024 * 1024
ROW_TILE = 512
FFN_TILE = 512
HALO = 16
FOX_TQ = 512
PAGES_PER_CHUNK = 16
SWA_SAMPLE_ROWS = 8
SCAN_BLOCK = 256
SCAN_ROWS = 128


def _cparams(*sem):
    return pltpu.CompilerParams(dimension_semantics=sem, vmem_limit_bytes=V7X_VMEM_LIMIT)


def _rms(x):
    return x * lax.rsqrt(jnp.mean(x * x, axis=-1, keepdims=True) + EPS)


def _dot(a, b):
    return jnp.dot(a, b, preferred_element_type=F32)


def _dot_nt(a, b):
    return lax.dot_general(a, b, (((1,), (1,)), ((), ())), preferred_element_type=F32)


def _split3(x):
    hi = x.astype(BF16)
    r = x - hi.astype(F32)
    mid = r.astype(BF16)
    lo = (r - mid.astype(F32)).astype(BF16)
    return hi, mid, lo


def _const_spec(shape):
    nd = len(shape)
    return pl.BlockSpec(shape, lambda *_: (0,) * nd)


def _weight_spec(w, layer=0):
    return pl.BlockSpec((None,) + w.shape[1:], lambda *_: (layer, 0, 0), pipeline_mode=pl.Buffered(1))


def _fox_proj_kernel(x_ref, g_ref, w_ref, bf_ref, qg_ref, kg_ref,
                     q_ref, k_ref, v_ref, kb_ref, vb_ref, lf_ref, *, nq, nkv):
    a = (_rms(x_ref[...]) * g_ref[...]).astype(BF16)
    qg = qg_ref[...]
    kg = kg_ref[...]
    chunk = lambda lo: _dot(a, w_ref[:, lo:lo + MXU_N].astype(BF16))
    for c in range(nq // MXU_N):
        z = chunk(c * MXU_N)
        for s in range(MXU_N // LANES):
            lo = c * MXU_N + s * LANES
            q_ref[:, lo:lo + LANES] = (_rms(z[:, s * LANES:(s + 1) * LANES]) * qg).astype(BF16)
    for c in range(nkv // MXU_N):
        z = chunk(nq + c * MXU_N)
        for s in range(MXU_N // LANES):
            lo = c * MXU_N + s * LANES
            kn = _rms(z[:, s * LANES:(s + 1) * LANES]) * kg
            k_ref[:, lo:lo + LANES] = kn
            kb_ref[:, lo:lo + LANES] = kn.astype(BF16)
    for c in range(nkv // MXU_N):
        z = chunk(nq + nkv + c * MXU_N)
        v_ref[:, c * MXU_N:(c + 1) * MXU_N] = z
        vb_ref[:, c * MXU_N:(c + 1) * MXU_N] = z.astype(BF16)
    zf = _dot(a, w_ref[:, nq + 2 * nkv:].astype(BF16)) + bf_ref[...]
    lf_ref[...] = -(jnp.maximum(-zf, 0.0) + jnp.log1p(jnp.exp(-jnp.abs(zf))))


def _fox_proj(x, g, w_in, bf, qg, kg, nq, nkv):
    m, d = x.shape
    nh = w_in.shape[2] - nq - 2 * nkv
    tm = min(ROW_TILE, m)
    row = lambda n: pl.BlockSpec((tm, n), lambda i: (i, 0))
    return pl.pallas_call(
        functools.partial(_fox_proj_kernel, nq=nq, nkv=nkv),
        grid=(m // tm,),
        in_specs=[row(d), _const_spec(g.shape), _weight_spec(w_in), _const_spec(bf.shape), _const_spec(qg.shape),
                  _const_spec(kg.shape)],
        out_specs=[row(nq), row(nkv), row(nkv), row(nkv), row(nkv), row(nh)],
        out_shape=[jax.ShapeDtypeStruct((m, nq), BF16), jax.ShapeDtypeStruct((m, nkv), F32),
                   jax.ShapeDtypeStruct((m, nkv), F32), jax.ShapeDtypeStruct((m, nkv), BF16),
                   jax.ShapeDtypeStruct((m, nkv), BF16), jax.ShapeDtypeStruct((m, nh), F32)],
        compiler_params=_cparams("arbitrary"),
        name="fox_proj",
    )(x, g, w_in, bf, qg, kg)


def _scan_kernel(x_ref, tri_ref, o_ref, *, suffix):
    tr, n = x_ref.shape
    nblk = n // SCAN_BLOCK
    tri = tri_ref[...]
    carry = jnp.zeros((tr, 1), F32)
    order = range(nblk - 1, -1, -1) if suffix else range(nblk)
    for blk in order:
        cols = slice(blk * SCAN_BLOCK, (blk + 1) * SCAN_BLOCK)
        x = x_ref[:, cols]
        r = _dot(jnp.concatenate(_split3(x), axis=0), tri)
        y = r[0:tr] + r[tr:2 * tr] + r[2 * tr:3 * tr] + carry
        if suffix:
            o_ref[:, cols] = y
            carry = y[:, 0:1] + x[:, 0:1]
        else:
            o_ref[:, cols] = -y
            carry = y[:, SCAN_BLOCK - 1:SCAN_BLOCK]


def _scan_lanes(x, suffix):
    rows, n = x.shape
    tr = min(SCAN_ROWS, rows)
    i = lax.broadcasted_iota(jnp.int32, (SCAN_BLOCK, SCAN_BLOCK), 0)
    j = lax.broadcasted_iota(jnp.int32, (SCAN_BLOCK, SCAN_BLOCK), 1)
    tri = jnp.where(i > j if suffix else i <= j, 1.0, 0.0).astype(BF16)
    blk = pl.BlockSpec((tr, n), lambda i: (i, 0))
    return pl.pallas_call(
        functools.partial(_scan_kernel, suffix=suffix),
        grid=(rows // tr,),
        in_specs=[blk, _const_spec(tri.shape)],
        out_specs=blk,
        out_shape=jax.ShapeDtypeStruct((rows, n), F32),
        compiler_params=_cparams("arbitrary"),
        name="logf_suffix" if suffix else "logf_prefix",
    )(x, tri)


def _fox_prompt_kernel(q_ref, k_ref, v_ref, b_ref, o_ref, qs_ref, m_ref, l_ref, acc_ref, *, group, hd):
    i = pl.program_id(2)
    t = q_ref.shape[0]
    rows = group * t
    for g in range(group):
        qs_ref[g * t:(g + 1) * t, :] = q_ref[:, g * hd:(g + 1) * hd]
    m_ref[...] = jnp.full(m_ref.shape, NEG, F32)
    l_ref[...] = jnp.zeros(l_ref.shape, F32)
    acc_ref[...] = jnp.zeros(acc_ref.shape, F32)

    def tile(kt, masked):
        k0 = pl.multiple_of(kt * t, t)
        s = _dot_nt(qs_ref[...], k_ref[pl.ds(k0, t), :]).reshape(group, t, t)
        s = s + (b_ref[0, 0, :, pl.ds(k0, t)] * LOG2E)[:, None, :]
        if masked:
            r = lax.broadcasted_iota(jnp.int32, (group, t, t), 1)
            c = lax.broadcasted_iota(jnp.int32, (group, t, t), 2)
            s = jnp.where(c <= r, s, NEG)
        s = s.reshape(rows, t)
        m_old = m_ref[...]
        m_new = jnp.maximum(m_old, jnp.max(s, axis=-1, keepdims=True))
        p = jnp.exp2(s - jnp.concatenate([m_new] * (t // LANES), axis=1))
        alpha = jnp.exp2(m_old - m_new)
        l_ref[...] = alpha * l_ref[...] + jnp.sum(p, axis=-1, keepdims=True)
        acc_ref[...] = alpha * acc_ref[...] + _dot(p.astype(BF16), v_ref[pl.ds(k0, t), :])
        m_ref[...] = m_new

    def body(kt, carry):
        tile(kt, False)
        return carry

    lax.fori_loop(0, i, body, 0)
    tile(i, True)
    o = acc_ref[...] / l_ref[...]
    for g in range(group):
        o_ref[:, g * hd:(g + 1) * hd] = o[g * t:(g + 1) * t].astype(o_ref.dtype)


def _fox_prompt_attn(q, kb, vb, bias, batch, seq, nkv, hd):
    assert hd == LANES
    group = q.shape[1] // (nkv * hd)
    t = FOX_TQ
    nq = seq // t
    return pl.pallas_call(
        functools.partial(_fox_prompt_kernel, group=group, hd=hd),
        grid=(batch, nkv, nq),
        in_specs=[pl.BlockSpec((t, group * hd), lambda b, h, i: (b * nq + i, h)),
                  pl.BlockSpec((seq, hd), lambda b, h, i: (b, h)),
                  pl.BlockSpec((seq, hd), lambda b, h, i: (b, h)),
                  pl.BlockSpec((1, 1, group, seq), lambda b, h, i: (b, h, 0, 0))],
        out_specs=pl.BlockSpec((t, group * hd), lambda b, h, i: (b * nq + i, h)),
        out_shape=jax.ShapeDtypeStruct(q.shape, BF16),
        scratch_shapes=[pltpu.VMEM((group * t, hd), BF16), pltpu.VMEM((group * t, LANES), F32),
                        pltpu.VMEM((group * t, LANES), F32), pltpu.VMEM((group * t, hd), F32)],
        compiler_params=_cparams("arbitrary", "arbitrary", "arbitrary"),
        name="fox_prompt_attn",
    )(q, kb, vb, bias)


def _fox_sample_kernel(pt_ref, q_ref, kn_ref, vn_ref, bn_ref, e_ref, tri_ref, kpool, vpool, lpool,
                       o_ref, kbuf, vbuf, lbuf, sem, *, nkv, hd, n_chunks, rows_q, group):
    b = pl.program_id(0)
    nb = pl.num_programs(0)
    page_rows = PAGE_SIZE * nkv
    lc = PAGES_PER_CHUNK * PAGE_SIZE
    nh = lbuf.shape[2]
    pages_per_block = SCAN_BLOCK // PAGE_SIZE
    nblk = lc // SCAN_BLOCK

    def start_chunk(bb, step, slot):
        c = n_chunks - 1 - step
        for p in range(PAGES_PER_CHUNK):
            page = pt_ref[bb, c * PAGES_PER_CHUNK + p]
            src = pl.ds(pl.multiple_of(page * page_rows, page_rows), page_rows)
            dst = pl.ds(p * page_rows, page_rows)
            pltpu.make_async_copy(kpool.at[src, :], kbuf.at[slot, dst, :], sem.at[0, slot]).start()
            pltpu.make_async_copy(vpool.at[src, :], vbuf.at[slot, dst, :], sem.at[1, slot]).start()
            pltpu.make_async_copy(lpool.at[page], lbuf.at[slot, p], sem.at[2, slot]).start()

    def wait_chunk(slot):
        for buf, s in ((kbuf, 0), (vbuf, 1), (lbuf, 2)):
            pltpu.make_async_copy(buf.at[slot], buf.at[slot], sem.at[s, slot]).wait()

    @pl.when(b == 0)
    def _():
        start_chunk(0, 0, 0)

    e = e_ref[...]
    tri = tri_ref[...]
    q = [q_ref[0, h] for h in range(nkv)]

    def chunk(step, carry):
        slot = lax.rem(b * n_chunks + step, 2)
        nxt_step = lax.rem(step + 1, n_chunks)
        nxt_b = jnp.where(step + 1 == n_chunks, b + 1, b)

        @pl.when(nxt_b < nb)
        def _():
            start_chunk(nxt_b, nxt_step, 1 - slot)

        wait_chunk(slot)

        xs = [jnp.concatenate([lbuf[slot, k * pages_per_block + p] for p in range(pages_per_block)], axis=1)
              for k in range(nblk)]
        r = _dot(jnp.concatenate([part for x in xs for part in _split3(x)], axis=0), tri)
        sfx_carry = carry[nkv]
        ys = [None] * nblk
        for k in range(nblk - 1, -1, -1):
            local = r[(3 * k) * nh:(3 * k + 1) * nh] + r[(3 * k + 1) * nh:(3 * k + 2) * nh] \
                + r[(3 * k + 2) * nh:(3 * k + 3) * nh]
            ys[k] = local + sfx_carry
            sfx_carry = sfx_carry + (local[:, 0:1] + xs[k][:, 0:1])
        bias = _dot(e, jnp.concatenate(_split3(jnp.concatenate(ys, axis=1) * LOG2E), axis=0))
        new = []
        for h in range(nkv):
            m_old, l_old, acc_old = carry[h]
            k = kbuf[slot, pl.ds(h, lc, stride=nkv), :].astype(BF16)
            v = vbuf[slot, pl.ds(h, lc, stride=nkv), :].astype(BF16)
            s = _dot_nt(q[h], k) + bias[h * rows_q:(h + 1) * rows_q]
            m_new = jnp.maximum(m_old, jnp.max(s, axis=-1, keepdims=True))
            p = jnp.exp2(s - m_new)
            alpha = jnp.exp2(m_old - m_new)
            new.append((m_new, alpha * l_old + jnp.sum(p, axis=-1, keepdims=True),
                        alpha * acc_old + _dot(p.astype(BF16), v)))
        return tuple(new) + (sfx_carry,)

    init = tuple((jnp.full((rows_q, 1), NEG, F32), jnp.zeros((rows_q, 1), F32),
                  jnp.zeros((rows_q, hd), F32)) for _ in range(nkv)) + (jnp.zeros((nh, 1), F32),)
    state = lax.fori_loop(0, n_chunks, chunk, init)

    tn = kn_ref.shape[1]
    bias_n = _dot_nt(e, bn_ref[0])
    r = lax.broadcasted_iota(jnp.int32, (rows_q, tn), 0)
    c = lax.broadcasted_iota(jnp.int32, (rows_q, tn), 1)
    ok = c * group <= r
    for h in range(nkv):
        m_old, l_old, acc_old = state[h]
        k = kn_ref[0, :, h * hd:(h + 1) * hd].astype(BF16)
        v = vn_ref[0, :, h * hd:(h + 1) * hd].astype(BF16)
        s = jnp.where(ok, _dot_nt(q[h], k) + bias_n[h * rows_q:(h + 1) * rows_q], NEG)
        m_new = jnp.maximum(m_old, jnp.max(s, axis=-1, keepdims=True))
        p = jnp.exp2(s - m_new)
        alpha = jnp.exp2(m_old - m_new)
        l_new = alpha * l_old + jnp.sum(p, axis=-1, keepdims=True)
        acc = alpha * acc_old + _dot(p.astype(BF16), v)
        o_ref[0, h] = (acc / l_new).astype(o_ref.dtype)


def _fox_sample_attn(page_table, q, k_new, v_new, bias_new3, emat, kpool, vpool, lpool, nkv, hd, group):
    nb, _, rows_q, _ = q.shape
    n_pages = page_table.shape[1]
    nh = lpool.shape[1]
    n_chunks = n_pages // PAGES_PER_CHUNK
    buf_rows = PAGES_PER_CHUNK * PAGE_SIZE * nkv
    tn = k_new.shape[1]
    i = lax.broadcasted_iota(jnp.int32, (SCAN_BLOCK, SCAN_BLOCK), 0)
    j = lax.broadcasted_iota(jnp.int32, (SCAN_BLOCK, SCAN_BLOCK), 1)
    tri = jnp.where(i > j, 1.0, 0.0).astype(BF16)
    const = lambda a: pl.BlockSpec(a.shape, lambda b, pt: (0,) * a.ndim)
    grid_spec = pltpu.PrefetchScalarGridSpec(
        num_scalar_prefetch=1,
        grid=(nb,),
        in_specs=[pl.BlockSpec((1, nkv, rows_q, hd), lambda b, pt: (b, 0, 0, 0)),
                  pl.BlockSpec((1, tn, nkv * hd), lambda b, pt: (b, 0, 0)),
                  pl.BlockSpec((1, tn, nkv * hd), lambda b, pt: (b, 0, 0)),
                  pl.BlockSpec((1,) + bias_new3.shape[1:], lambda b, pt: (b, 0, 0)),
                  const(emat), const(tri),
                  pl.BlockSpec(memory_space=pl.ANY),
                  pl.BlockSpec(memory_space=pl.ANY),
                  pl.BlockSpec(memory_space=pl.ANY)],
        out_specs=pl.BlockSpec((1, nkv, rows_q, hd), lambda b, pt: (b, 0, 0, 0)),
        scratch_shapes=[pltpu.VMEM((2, buf_rows, hd), F32), pltpu.VMEM((2, buf_rows, hd), F32),
                        pltpu.VMEM((2, PAGES_PER_CHUNK, nh, PAGE_SIZE), F32),
                        pltpu.SemaphoreType.DMA((3, 2))],
    )
    return pl.pallas_call(
        functools.partial(_fox_sample_kernel, nkv=nkv, hd=hd, n_chunks=n_chunks, rows_q=rows_q, group=group),
        grid_spec=grid_spec,
        out_shape=jax.ShapeDtypeStruct(q.shape, BF16),
        compiler_params=_cparams("arbitrary"),
        name="fox_sample_attn",
    )(page_table, q, k_new, v_new, bias_new3, emat, tri, kpool, vpool, lpool)


def _out_proj_kernel(o_ref, h_ref, w_ref, g_ref, h1_ref, c_ref):
    o = o_ref[...]
    tn = 512
    for c in range(h_ref.shape[1] // tn):
        cs = slice(c * tn, (c + 1) * tn)
        h1_ref[:, cs] = h_ref[:, cs] + _dot(o, w_ref[:, cs].astype(BF16))
    c_ref[...] = (_rms(h1_ref[...]) * g_ref[...]).astype(BF16)


def _out_proj(o, h, w, g):
    m, d = h.shape
    tm = min(ROW_TILE, m)
    row = lambda n: pl.BlockSpec((tm, n), lambda i: (i, 0))
    return pl.pallas_call(
        _out_proj_kernel,
        grid=(m // tm,),
        in_specs=[row(o.shape[1]), row(d), _weight_spec(w), _const_spec(g.shape)],
        out_specs=[row(d), row(d)],
        out_shape=[jax.ShapeDtypeStruct((m, d), F32), jax.ShapeDtypeStruct((m, d), BF16)],
        compiler_params=_cparams("arbitrary"),
        name="attn_out_proj",
    )(o, h, w, g)


def _ffn_epilogue(j, h_ref, gn_ref, acc_ref, h2_ref, n_ref):
    @pl.when(j == pl.num_programs(1) - 1)
    def _():
        h2 = h_ref[...] + acc_ref[...]
        h2_ref[...] = h2
        n_ref[...] = (_rms(h2) * gn_ref[...]).astype(BF16)


def _ffn_prompt_kernel(c_ref, wg_ref, wv_ref, cwg_ref, cwv_ref, cbg_ref, cbv_ref, wo_ref, h_ref, gn_ref,
                       h2_ref, n_ref, tg_ref, tv_ref, acc_ref, ug_ref, uv_ref, pg_ref, pv_ref, *, tiles_per_seq):
    i, j = pl.program_id(0), pl.program_id(1)
    tm = c_ref.shape[0]
    first = lax.rem(i, tiles_per_seq) == 0

    @pl.when(j == 0)
    def _():
        acc_ref[...] = jnp.zeros(acc_ref.shape, F32)

    @pl.when((i == 0) & (j == 0))
    def _():
        pg_ref[...] = jnp.zeros(pg_ref.shape, F32)
        pv_ref[...] = jnp.zeros(pv_ref.shape, F32)

    x = c_ref[...]
    for u_ref, p_ref, w_ref, t_ref in ((ug_ref, pg_ref, wg_ref, tg_ref), (uv_ref, pv_ref, wv_ref, tv_ref)):
        u_ref[0:HALO, :] = jnp.where(first, 0.0, p_ref[j])
        u_ref[HALO:, :] = _dot(x, w_ref[...])
        tail = u_ref[tm:tm + HALO, :]
        p_ref[j] = tail
        t_ref[...] = tail

    def conv(u_ref, cw_ref, cb_ref):
        y = cb_ref[...]
        for k in range(CONV_W):
            y = y + cw_ref[k:k + 1, :] * u_ref[pl.ds(HALO - (CONV_W - 1) + k, tm), :]
        return y

    act = (jax.nn.gelu(conv(ug_ref, cwg_ref, cbg_ref), approximate=True) * conv(uv_ref, cwv_ref, cbv_ref)).astype(BF16)
    acc_ref[...] += _dot(act, wo_ref[...])
    _ffn_epilogue(j, h_ref, gn_ref, acc_ref, h2_ref, n_ref)


def _ffn_sample_kernel(c_ref, pg_ref, pv_ref, wg_ref, wv_ref, cwg_ref, cwv_ref, cbg_ref, cbv_ref, wo_ref,
                       h_ref, gn_ref, h2_ref, n_ref, tg_ref, tv_ref, acc_ref):
    j = pl.program_id(1)
    nb = pg_ref.shape[1]
    nt = c_ref.shape[0] // nb

    @pl.when(j == 0)
    def _():
        acc_ref[...] = jnp.zeros(acc_ref.shape, F32)

    x = c_ref[...]

    def conv(u, p_ref, cw_ref, cb_ref, t_ref):
        slabs = [p_ref[k] for k in range(CONV_W - 1)] + [u[t * nb:(t + 1) * nb] for t in range(nt)]
        for k in range(CONV_W - 1):
            t_ref[k] = slabs[nt + k]
        y = cb_ref[...]
        for k in range(CONV_W):
            y = y + cw_ref[k:k + 1, :] * jnp.concatenate(slabs[k:k + nt], axis=0)
        return y

    g = conv(_dot(x, wg_ref[...]), pg_ref, cwg_ref, cbg_ref, tg_ref)
    v = conv(_dot(x, wv_ref[...]), pv_ref, cwv_ref, cbv_ref, tv_ref)
    act = (jax.nn.gelu(g, approximate=True) * v).astype(BF16)
    acc_ref[...] += _dot(act, wo_ref[...])
    _ffn_epilogue(j, h_ref, gn_ref, acc_ref, h2_ref, n_ref)


def _conv_ffn(c, h, layer, w_in, conv_w, conv_b, w_out, g_next, past, tiles_per_seq):
    m, d = h.shape
    dff = w_out.shape[1]
    tf = FFN_TILE
    nf = dff // tf
    tm = min(ROW_TILE, m)
    wspecs = [pl.BlockSpec((None, d, tf), lambda i, j: (layer, 0, j)),
              pl.BlockSpec((None, d, tf), lambda i, j: (layer, 0, nf + j)),
              pl.BlockSpec((None, CONV_W, tf), lambda i, j: (layer, 0, j)),
              pl.BlockSpec((None, CONV_W, tf), lambda i, j: (layer, 0, nf + j)),
              pl.BlockSpec((None, 1, tf), lambda i, j: (layer, 0, j)),
              pl.BlockSpec((None, 1, tf), lambda i, j: (layer, 0, nf + j)),
              pl.BlockSpec((None, tf, d), lambda i, j: (layer, j, 0))]
    wargs = [w_in, w_in, conv_w, conv_w, conv_b, conv_b, w_out]
    row = pl.BlockSpec((tm, d), lambda i, j: (i, 0))
    tail_specs = [row, pl.BlockSpec((1, d), lambda i, j: (0, 0))]
    if past is None:
        head_specs = [row]
        head_args = [c]
        kern = functools.partial(_ffn_prompt_kernel, tiles_per_seq=tiles_per_seq)
        scratch = [pltpu.VMEM((tm + HALO, tf), F32)] * 2 + [pltpu.VMEM((nf, HALO, tf), F32)] * 2
        state = jax.ShapeDtypeStruct((m // tm, HALO, dff), F32)
        state_spec = pl.BlockSpec((None, HALO, tf), lambda i, j: (i, 0, j))
    else:
        scratch = []
        nb = past.shape[2]
        head_specs = [row, pl.BlockSpec((None, CONV_W - 1, nb, tf), lambda i, j: (layer, 0, 0, j)),
                      pl.BlockSpec((None, CONV_W - 1, nb, tf), lambda i, j: (layer, 0, 0, nf + j))]
        head_args = [c, past, past]
        kern = _ffn_sample_kernel
        state = jax.ShapeDtypeStruct((CONV_W - 1, nb, dff), F32)
        state_spec = pl.BlockSpec((CONV_W - 1, nb, tf), lambda i, j: (0, 0, j))
    return pl.pallas_call(
        kern,
        grid=(m // tm, nf),
        in_specs=head_specs + wspecs + tail_specs,
        out_specs=[row, row, state_spec, state_spec],
        out_shape=[jax.ShapeDtypeStruct((m, d), F32), jax.ShapeDtypeStruct((m, d), BF16), state, state],
        scratch_shapes=[pltpu.VMEM((tm, d), F32)] + scratch,
        compiler_params=_cparams("arbitrary", "arbitrary"),
        name="conv_ffn_prompt" if past is None else "conv_ffn_sample",
    )(*head_args, *wargs, h, g_next)


def _ple_kernel(h_ref, n_ref, p_ref, wp_ref, wg_ref, *rest, n_norms):
    g_refs = rest[:n_norms]
    h3_ref = rest[n_norms]
    a_refs = rest[n_norms + 1:]
    n = n_ref[...]
    p = p_ref[...].astype(BF16)
    d = h_ref.shape[1]
    tn = 512
    for c in range(d // tn):
        cs = slice(c * tn, (c + 1) * tn)
        gate = jax.nn.sigmoid(_dot(n, wg_ref[:, cs].astype(BF16)))
        h3_ref[:, cs] = h_ref[:, cs] + _dot(p, wp_ref[:, cs].astype(BF16)) * gate
    if n_norms:
        xhat = _rms(h3_ref[...])
        for g_ref, a_ref in zip(g_refs, a_refs):
            a_ref[...] = (xhat * g_ref[...]).astype(BF16)


def _ple(h, n, p, layer, wp, wg, gains):
    m, d = h.shape
    tm = min(ROW_TILE, m)
    row = lambda w: pl.BlockSpec((tm, w), lambda i: (i, 0))
    k = len(gains)
    outs = pl.pallas_call(
        functools.partial(_ple_kernel, n_norms=k),
        grid=(m // tm,),
        in_specs=[row(d), row(d), pl.BlockSpec((None, tm, p.shape[2]), lambda i: (layer, i, 0)),
                  _weight_spec(wp, layer), _weight_spec(wg, layer)]
                 + [_const_spec(g.shape) for g in gains],
        out_specs=[row(d)] * (k + 1),
        out_shape=[jax.ShapeDtypeStruct((m, d), F32)] + [jax.ShapeDtypeStruct((m, d), BF16)] * k,
        compiler_params=_cparams("arbitrary"),
        name="ple",
    )(h, n, p, wp, wg, *gains)
    return outs


def _swa_proj_kernel(a_ref, akv_ref, wq_ref, wkv_ref, qg_ref, kg_ref, q_ref, k_ref, v_ref, *, nkv, hd):
    a = a_ref[...]
    qg = qg_ref[...]
    lane = lax.broadcasted_iota(jnp.int32, (a.shape[0], LANES), 1)
    low = lane < hd
    for c in range(wq_ref.shape[1] // MXU_N):
        z = _dot(a, wq_ref[:, c * MXU_N:(c + 1) * MXU_N].astype(BF16))
        for s in range(MXU_N // LANES):
            zz = z[:, s * LANES:(s + 1) * LANES]
            sq = zz * zz
            s_lo = jnp.sum(jnp.where(low, sq, 0.0), axis=-1, keepdims=True)
            s_hi = jnp.sum(jnp.where(low, 0.0, sq), axis=-1, keepdims=True)
            inv = jnp.where(low, lax.rsqrt(s_lo / hd + EPS), lax.rsqrt(s_hi / hd + EPS))
            lo = c * MXU_N + s * LANES
            q_ref[:, lo:lo + LANES] = (zz * inv * qg).astype(BF16)
    akv = akv_ref[...]
    kg = kg_ref[...]
    for c in range(nkv // MXU_N):
        z = _dot(akv, wkv_ref[:, c * MXU_N:(c + 1) * MXU_N])
        for s in range(MXU_N // LANES):
            lo = c * MXU_N + s * LANES
            k_ref[:, lo:lo + LANES] = _rms(z[:, s * LANES:(s + 1) * LANES]) * kg
    for c in range(nkv // MXU_N):
        v_ref[:, c * MXU_N:(c + 1) * MXU_N] = _dot(akv, wkv_ref[:, nkv + c * MXU_N:nkv + (c + 1) * MXU_N])


def _swa_proj(a, akv, wq, wkv_dup, qg, kg, hd):
    m, d = a.shape
    nkv = wkv_dup.shape[2] // 2
    nq = wq.shape[2]
    tm = min(ROW_TILE, m)
    row = lambda n: pl.BlockSpec((tm, n), lambda i: (i, 0))
    return pl.pallas_call(
        functools.partial(_swa_proj_kernel, nkv=nkv, hd=hd),
        grid=(m // tm,),
        in_specs=[row(d), row(d), _weight_spec(wq), _weight_spec(wkv_dup),
                  _const_spec(qg.shape), _const_spec(kg.shape)],
        out_specs=[row(nq), row(nkv), row(nkv)],
        out_shape=[jax.ShapeDtypeStruct((m, nq), BF16), jax.ShapeDtypeStruct((m, nkv), F32),
                   jax.ShapeDtypeStruct((m, nkv), F32)],
        compiler_params=_cparams("arbitrary"),
        name="swa_proj",
    )(a, akv, wq, wkv_dup, qg, kg)


def _swa_prompt_kernel(slope_ref, sink_ref, q_ref, kp_ref, kc_ref, vp_ref, vc_ref, o_ref, *, nkv, hd):
    i = pl.program_id(1)
    w = q_ref.shape[0]
    pairs = q_ref.shape[1] // (nkv * LANES)
    lane = lax.broadcasted_iota(jnp.int32, (w, LANES), 1)
    low = lane < hd
    r = lax.broadcasted_iota(jnp.int32, (w, 2 * w), 0)
    c = lax.broadcasted_iota(jnp.int32, (w, 2 * w), 1)
    dist = r + w - c
    valid = (dist >= 0) & (dist < w) & ((c >= w) | (i > 0))
    base = jnp.where(valid, dist.astype(F32), MASK_DIST)
    top = lax.broadcasted_iota(jnp.int32, (2 * w, 1), 0) < w
    for h in range(nkv):
        k = jnp.concatenate([kp_ref[:, h * LANES:(h + 1) * LANES], kc_ref[:, h * LANES:(h + 1) * LANES]],
                            axis=0).astype(BF16)
        v = jnp.concatenate([vp_ref[:, h * LANES:(h + 1) * LANES], vc_ref[:, h * LANES:(h + 1) * LANES]],
                            axis=0).astype(BF16)
        for pr in range(pairs):
            col = (h * pairs + pr) * LANES
            h0 = 2 * (h * pairs + pr)
            qp = q_ref[:, col:col + LANES]
            rows = jnp.concatenate([jnp.where(low, qp, jnp.zeros_like(qp)),
                                    jnp.where(low, jnp.zeros_like(qp), qp)], axis=0)
            s = _dot_nt(rows, k) - jnp.concatenate([slope_ref[h0] * base, slope_ref[h0 + 1] * base], axis=0)
            sink = jnp.where(top, sink_ref[h0], sink_ref[h0 + 1])
            m = jnp.maximum(jnp.max(s, axis=-1, keepdims=True), sink)
            p = jnp.exp2(s - m)
            den = jnp.sum(p, axis=-1, keepdims=True) + jnp.exp2(sink - m)
            o = _dot(p.astype(BF16), v) / den
            o_ref[:, col:col + LANES] = jnp.where(low, o[:w], o[w:]).astype(o_ref.dtype)


def _swa_prompt_attn(q, kdup, vdup, slopes, sinks, batch, seq, nkv, hd):
    nblk = seq // WINDOW
    dq = q.shape[1]
    dk = kdup.shape[1]
    cur = lambda b, i: (b * nblk + i, 0)
    prev = lambda b, i: (b * nblk + jnp.maximum(i - 1, 0), 0)
    smem = pl.BlockSpec(memory_space=pltpu.SMEM)
    return pl.pallas_call(
        functools.partial(_swa_prompt_kernel, nkv=nkv, hd=hd),
        grid=(batch, nblk),
        in_specs=[smem, smem, pl.BlockSpec((WINDOW, dq), cur), pl.BlockSpec((WINDOW, dk), prev),
                  pl.BlockSpec((WINDOW, dk), cur), pl.BlockSpec((WINDOW, dk), prev),
                  pl.BlockSpec((WINDOW, dk), cur)],
        out_specs=pl.BlockSpec((WINDOW, dq), cur),
        out_shape=jax.ShapeDtypeStruct(q.shape, BF16),
        compiler_params=_cparams("arbitrary", "arbitrary"),
        name="swa_prompt_attn",
    )(slopes, sinks, q, kdup, kdup, vdup, vdup)


def _swa_sample_kernel(q_ref, kc_ref, vc_ref, kn_ref, vn_ref, slope_ref, sink_ref, o_ref, *, heads):
    rows = q_ref.shape[1]
    w = kc_ref.shape[2]
    tn = kn_ref.shape[1]
    slope = slope_ref[...]
    sink = sink_ref[...]
    t_p = lax.broadcasted_iota(jnp.int32, (rows, w), 0) // heads
    j_p = lax.broadcasted_iota(jnp.int32, (rows, w), 1)
    dist_p = t_p + w - j_p
    bias_p = jnp.where(dist_p < WINDOW, slope * dist_p.astype(F32), -NEG)
    t_n = lax.broadcasted_iota(jnp.int32, (rows, tn), 0) // heads
    j_n = lax.broadcasted_iota(jnp.int32, (rows, tn), 1)
    dist_n = t_n - j_n
    bias_n = jnp.where(dist_n >= 0, slope * dist_n.astype(F32), -NEG)
    for b in range(q_ref.shape[0]):
        q = q_ref[b]
        s_p = _dot(q, kc_ref[b].astype(BF16)) - bias_p
        s_n = _dot_nt(q, kn_ref[b].astype(BF16)) - bias_n
        m = jnp.maximum(jnp.maximum(jnp.max(s_p, axis=-1, keepdims=True), jnp.max(s_n, axis=-1, keepdims=True)),
                        sink)
        p_p = jnp.exp2(s_p - m)
        p_n = jnp.exp2(s_n - m)
        den = jnp.sum(p_p, axis=-1, keepdims=True) + jnp.sum(p_n, axis=-1, keepdims=True) + jnp.exp2(sink - m)
        o = _dot_nt(p_p.astype(BF16), vc_ref[b].astype(BF16)) + _dot(p_n.astype(BF16), vn_ref[b].astype(BF16))
        o_ref[b] = o / den


def _swa_sample_attn(q, kct, vct, kn, vn, slope_col, sink_col, heads):
    nb, rows, dk = q.shape
    g = SWA_SAMPLE_ROWS
    blk = lambda a: pl.BlockSpec((g,) + a.shape[1:], lambda b: (b, 0, 0))
    return pl.pallas_call(
        functools.partial(_swa_sample_kernel, heads=heads),
        grid=(nb // g,),
        in_specs=[blk(q), blk(kct), blk(vct), blk(kn), blk(vn), _const_spec(slope_col.shape),
                  _const_spec(sink_col.shape)],
        out_specs=blk(q),
        out_shape=jax.ShapeDtypeStruct((nb, rows, dk), F32),
        compiler_params=_cparams("arbitrary"),
        name="swa_sample_attn",
    )(q, kct, vct, kn, vn, slope_col, sink_col)


def kernel(x_prompt, x_sample, cache_fox_k, cache_fox_v, cache_fox_logf, cache_win_k, cache_win_v, state_conv, page_table, p_prompt, p_sample, norm_attn_g, norm_ffn_g, norm_ple_g, fox_w_in, fox_b_f, fox_q_norm_g, fox_k_norm_g, fox_w_out, kv_norm_g, swa_w_kv, swa_k_norm_g, swa_w_q, swa_q_norm_g, swa_sinks, swa_w_out, ffn_w_in, ffn_conv_w, ffn_conv_b, ffn_w_out, ple_w_proj, ple_w_gate):
    B, S, D = x_prompt.shape
    NB, T, _ = x_sample.shape
    HD_A = fox_q_norm_g.shape[-1]
    H_A = fox_b_f.shape[-1]
    KVH_A = cache_fox_k.shape[3]
    G_A = H_A // KVH_A
    HD_B = swa_q_norm_g.shape[-1]
    H_B = swa_sinks.shape[-1]
    KVH_B = cache_win_k.shape[2]
    G_B = H_B // KVH_B
    WIN = cache_win_k.shape[1]
    L = page_table.shape[1] * PAGE_SIZE
    DFF = ffn_w_out.shape[1]
    NQ_A, NK_A = H_A * HD_A, KVH_A * HD_A
    row2 = lambda g: g.reshape(1, -1)

    bf_a = fox_b_f
    qg_a = row2(fox_q_norm_g[0] * (HD_A ** -0.5 * LOG2E))
    kg_a = row2(fox_k_norm_g[0])
    dup = lambda w: jnp.repeat(w.reshape(D, KVH_B, 1, HD_B), LANES // HD_B, axis=2).reshape(D, KVH_B * LANES)
    wk_b, wv_b = jnp.split(swa_w_kv, 2, axis=-1)
    wkv_b = jnp.concatenate([dup(wk_b), dup(wv_b)], axis=1).astype(BF16)[None]
    qg_b = row2(jnp.tile(swa_q_norm_g[0] * (HD_B ** -0.5 * LOG2E), LANES // HD_B))
    kg_b = row2(jnp.tile(swa_k_norm_g, LANES // HD_B))
    ffn_wi = ffn_w_in.astype(BF16)
    ffn_wo = ffn_w_out.astype(BF16)
    slopes = jnp.exp2(-8.0 * jnp.arange(1, H_B + 1, dtype=F32) / H_B) * LOG2E
    sinks = swa_sinks[0] * LOG2E

    xp = x_prompt.reshape(B * S, D)
    xs = x_sample.transpose(1, 0, 2).reshape(T * NB, D)
    pp = p_prompt.reshape(2, B * S, -1)
    ps = p_sample.transpose(0, 2, 1, 3).reshape(2, T * NB, -1)
    past = state_conv.transpose(0, 2, 1, 3)

    g0 = row2(norm_attn_g[0])
    q_p, k_p, v_p, kb_p, vb_p, lf_p = _fox_proj(xp, g0, fox_w_in, bf_a, qg_a, kg_a, NQ_A, NK_A)
    q_s, k_s, v_s, _, _, lf_s = _fox_proj(xs, g0, fox_w_in, bf_a, qg_a, kg_a, NQ_A, NK_A)

    bias_p = _scan_lanes(lf_p.reshape(B, S, H_A).transpose(0, 2, 1).reshape(B * H_A, S), suffix=False)
    o_p = _fox_prompt_attn(q_p, kb_p, vb_p, bias_p.reshape(B, KVH_A, G_A, S), B, S, KVH_A, HD_A)

    tn = 16
    lf_new = lf_s.reshape(T, NB, H_A).transpose(1, 2, 0).reshape(NB * H_A, T)
    bias_new = _scan_lanes(jnp.pad(lf_new, ((0, 0), (0, SCAN_BLOCK - T))), suffix=False)[:, :tn]
    bias_new3 = jnp.concatenate(_split3(bias_new.reshape(NB, H_A, tn).transpose(0, 2, 1) * LOG2E), axis=-1)
    rows_q = T * G_A
    head_of_row = (jnp.arange(KVH_A)[:, None] * G_A + jnp.arange(rows_q)[None, :] % G_A).reshape(-1)
    emat = jnp.tile(jax.nn.one_hot(head_of_row, H_A, dtype=BF16), (1, 3))
    qs_b = q_s.reshape(T, NB, KVH_A, G_A, HD_A).transpose(1, 2, 0, 3, 4).reshape(NB, KVH_A, rows_q, HD_A)
    pad_new = lambda a: jnp.pad(a.reshape(T, NB, NK_A).transpose(1, 0, 2), ((0, 0), (0, tn - T), (0, 0)))
    kpool = cache_fox_k.reshape(-1, HD_A)
    vpool = cache_fox_v.reshape(-1, HD_A)
    o_s = _fox_sample_attn(page_table, qs_b, pad_new(k_s), pad_new(v_s), bias_new3, emat, kpool, vpool,
                           cache_fox_logf.reshape(cache_fox_logf.shape[1:]).transpose(0, 2, 1), KVH_A, HD_A, G_A)
    o_s = o_s.reshape(NB, KVH_A, T, G_A, HD_A).transpose(2, 0, 1, 3, 4).reshape(T * NB, NQ_A)

    gf0 = row2(norm_ffn_g[0])
    h_p, c_p = _out_proj(o_p, xp, fox_w_out, gf0)
    h_s, c_s = _out_proj(o_s, xs, fox_w_out, gf0)

    gp0 = row2(norm_ple_g[0])
    tiles_per_seq = S // ROW_TILE
    conv_b = ffn_conv_b[:, None, :]
    h_p, n_p, *state_p0 = _conv_ffn(c_p, h_p, 0, ffn_wi, ffn_conv_w, conv_b, ffn_wo, gp0, None, tiles_per_seq)
    h_s, n_s, *state_s0 = _conv_ffn(c_s, h_s, 0, ffn_wi, ffn_conv_w, conv_b, ffn_wo, gp0, past, 1)

    gq1, gkv = row2(norm_attn_g[1]), row2(kv_norm_g)
    h_p, a_p, akv_p = _ple(h_p, n_p, pp, 0, ple_w_proj, ple_w_gate, [gq1, gkv])
    h_s, a_s, akv_s = _ple(h_s, n_s, ps, 0, ple_w_proj, ple_w_gate, [gq1, gkv])

    q1_p, kd_p, vd_p = _swa_proj(a_p, akv_p, swa_w_q, wkv_b, qg_b, kg_b, HD_B)
    q1_s, kd_s, vd_s = _swa_proj(a_s, akv_s, swa_w_q, wkv_b, qg_b, kg_b, HD_B)
    undup = lambda a: a.reshape(a.shape[0], KVH_B, LANES // HD_B, HD_B)[:, :, 0]
    o1_p = _swa_prompt_attn(q1_p, kd_p, vd_p, slopes, sinks, B, S, KVH_B, HD_B)

    ksh_s = undup(kd_s).reshape(T, NB, KVH_B * HD_B).transpose(1, 0, 2)
    vsh_s = undup(vd_s).reshape(T, NB, KVH_B * HD_B).transpose(1, 0, 2)
    q1h = q1_s.reshape(T, NB, H_B, 1, HD_B).transpose(1, 0, 2, 3, 4)
    slab = (jnp.arange(H_B)[:, None] // G_B == jnp.arange(KVH_B)[None, :])[None, None, :, :, None]
    q1x = jnp.where(slab, q1h, jnp.zeros((), BF16)).reshape(NB, T * H_B, KVH_B * HD_B)
    padn = lambda a: jnp.pad(a, ((0, 0), (0, tn - T), (0, 0)))
    slope_col = jnp.tile(slopes, T).reshape(T * H_B, 1)
    sink_col = jnp.tile(sinks, T).reshape(T * H_B, 1)
    feature_major = lambda a: a.transpose(0, 2, 3, 1).reshape(NB, KVH_B * HD_B, WIN)
    o1x = _swa_sample_attn(q1x, feature_major(cache_win_k), feature_major(cache_win_v),
                           padn(ksh_s), padn(vsh_s), slope_col, sink_col, H_B)
    o1x = o1x.reshape(NB, T, H_B, KVH_B, HD_B)
    o1_s = jnp.sum(jnp.where(slab, o1x, 0.0), axis=3)
    o1_s = o1_s.transpose(1, 0, 2, 3).reshape(T * NB, H_B * HD_B).astype(BF16)

    gf1 = row2(norm_ffn_g[1])
    h_p, c_p = _out_proj(o1_p, h_p, swa_w_out, gf1)
    h_s, c_s = _out_proj(o1_s, h_s, swa_w_out, gf1)

    gp1 = row2(norm_ple_g[1])
    h_p, n_p, *state_p1 = _conv_ffn(c_p, h_p, 1, ffn_wi, ffn_conv_w, conv_b, ffn_wo, gp1, None, tiles_per_seq)
    h_s, n_s, *state_s1 = _conv_ffn(c_s, h_s, 1, ffn_wi, ffn_conv_w, conv_b, ffn_wo, gp1, past, 1)

    (y_p,) = _ple(h_p, n_p, pp, 1, ple_w_proj, ple_w_gate, [])
    (y_s,) = _ple(h_s, n_s, ps, 1, ple_w_proj, ple_w_gate, [])

    to_bt = lambda a, *tail: a.reshape((T, NB) + tail).swapaxes(0, 1)
    y_prompt = y_p.reshape(B, S, D)
    y_sample = to_bt(y_s, D)
    fk_p = k_p.reshape(1, B, S, KVH_A, HD_A)
    fv_p = v_p.reshape(1, B, S, KVH_A, HD_A)
    flf_p = lf_p.reshape(1, B, S, H_A)
    last_win = lambda a: undup(a.reshape(B, S, -1)[:, S - WIN:].reshape(B * WIN, -1)).reshape(B, WIN, KVH_B, HD_B)
    win_k_p = last_win(kd_p)
    win_v_p = last_win(vd_p)
    conv_p = jnp.stack([jnp.concatenate([t[tiles_per_seq - 1::tiles_per_seq, HALO - (CONV_W - 1):] for t in st],
                                        axis=-1) for st in (state_p0, state_p1)])
    conv_s = jnp.stack([jnp.concatenate(st, axis=-1).swapaxes(0, 1) for st in (state_s0, state_s1)])
    fk_s = to_bt(k_s, KVH_A, HD_A)[None]
    fv_s = to_bt(v_s, KVH_A, HD_A)[None]
    flf_s = to_bt(lf_s, H_A)[None]
    win_k_s = jnp.concatenate([cache_win_k, ksh_s.reshape(NB, T, KVH_B, HD_B)], axis=1)[:, -WIN:]
    win_v_s = jnp.concatenate([cache_win_v, vsh_s.reshape(NB, T, KVH_B, HD_B)], axis=1)[:, -WIN:]
    return (y_prompt, y_sample, fk_p, fv_p, flf_p, win_k_p, win_v_p, conv_p,
            fk_s, fv_s, flf_s, win_k_s, win_v_s, conv_s)
```

```python
import functools

import jax
import jax.numpy as jnp
from jax import lax
from jax.experimental import pallas as pl
from jax.experimental.pallas import tpu as pltpu

F32 = jnp.float32
BF16 = jnp.bfloat16

EPS = 1e-6
PAGE_SIZE = 128
WINDOW = 128
CONV_W = 3
NEG = -1e30
MASK_DIST = 1e9
LOG2E = 1.4426950408889634
LANES = 128
MXU_N = 256
V7X_VMEM_LIMIT = 56 * 1024 * 1024
ROW_TILE = 512
FFN_ROWS = 512
FFN_TILE = 512
HALO = 16
FOX_TQ = 512
PAGES_PER_CHUNK = 16
DECODE_SLOTS = 3
SWA_SAMPLE_ROWS = 8
SCAN_BLOCK = 256
SCAN_ROWS = 128


def _cparams(*sem):
    return pltpu.CompilerParams(dimension_semantics=sem, vmem_limit_bytes=V7X_VMEM_LIMIT)


def _rms(x):
    return x * lax.rsqrt(jnp.mean(x * x, axis=-1, keepdims=True) + EPS)


def _dot(a, b):
    return jnp.dot(a, b, preferred_element_type=F32)


def _dot_nt(a, b):
    return lax.dot_general(a, b, (((1,), (1,)), ((), ())), preferred_element_type=F32)


def _split3(x):
    hi = x.astype(BF16)
    r = x - hi.astype(F32)
    mid = r.astype(BF16)
    lo = (r - mid.astype(F32)).astype(BF16)
    return hi, mid, lo


def _const_spec(shape):
    nd = len(shape)
    return pl.BlockSpec(shape, lambda *_: (0,) * nd)


def _weight_spec(w, layer=0):
    return pl.BlockSpec((None,) + w.shape[1:], lambda *_: (layer, 0, 0), pipeline_mode=pl.Buffered(1))


def _fox_proj_kernel(x_ref, g_ref, w_ref, bf_ref, qg_ref, kg_ref,
                     q_ref, k_ref, v_ref, kb_ref, vb_ref, lf_ref, *, nq, nkv):
    a = (_rms(x_ref[...]) * g_ref[...]).astype(BF16)
    tm = a.shape[0]
    heads = nkv // LANES
    qg = qg_ref[...]
    kg = kg_ref[...]
    chunk = lambda lo: _dot_nt(a, w_ref[lo:lo + MXU_N, :].astype(BF16))
    for c in range(nq // MXU_N):
        z = chunk(c * MXU_N)
        for s in range(MXU_N // LANES):
            lo = c * MXU_N + s * LANES
            q_ref[:, lo:lo + LANES] = (_rms(z[:, s * LANES:(s + 1) * LANES]) * qg).astype(BF16)
    for c in range(nkv // MXU_N):
        z = chunk(nq + c * MXU_N)
        for s in range(MXU_N // LANES):
            lo = c * MXU_N + s * LANES
            kn = _rms(z[:, s * LANES:(s + 1) * LANES]) * kg
            k_ref[pl.ds(lo // LANES, tm, stride=heads), :] = kn
            kb_ref[:, lo:lo + LANES] = kn.astype(BF16)
    for c in range(nkv // MXU_N):
        z = chunk(nq + nkv + c * MXU_N)
        for s in range(MXU_N // LANES):
            lo = c * MXU_N + s * LANES
            v_ref[pl.ds(lo // LANES, tm, stride=heads), :] = z[:, s * LANES:(s + 1) * LANES]
        vb_ref[:, c * MXU_N:(c + 1) * MXU_N] = z.astype(BF16)
    zf = _dot_nt(a, w_ref[nq + 2 * nkv:, :].astype(BF16)) + bf_ref[...]
    lf_ref[...] = -(jnp.maximum(-zf, 0.0) + jnp.log1p(jnp.exp(-jnp.abs(zf))))


def _fox_proj(x, g, w_in, bf, qg, kg, nq, nkv):
    m, d = x.shape
    nh = w_in.shape[1] - nq - 2 * nkv
    heads = nkv // LANES
    tm = min(ROW_TILE, m)
    row = lambda n: pl.BlockSpec((tm, n), lambda i: (i, 0))
    kv_f32 = pl.BlockSpec((tm * heads, LANES), lambda i: (i, 0))
    return pl.pallas_call(
        functools.partial(_fox_proj_kernel, nq=nq, nkv=nkv),
        grid=(m // tm,),
        in_specs=[row(d), _const_spec(g.shape), _weight_spec(w_in), _const_spec(bf.shape), _const_spec(qg.shape),
                  _const_spec(kg.shape)],
        out_specs=[row(nq), kv_f32, kv_f32, row(nkv), row(nkv), row(nh)],
        out_shape=[jax.ShapeDtypeStruct((m, nq), BF16), jax.ShapeDtypeStruct((m * heads, LANES), F32),
                   jax.ShapeDtypeStruct((m * heads, LANES), F32), jax.ShapeDtypeStruct((m, nkv), BF16),
                   jax.ShapeDtypeStruct((m, nkv), BF16), jax.ShapeDtypeStruct((m, nh), F32)],
        compiler_params=_cparams("arbitrary"),
        name="fox_proj",
    )(x, g, w_in, bf, qg, kg)


def _scan_kernel(x_ref, tri_ref, o_ref, *, suffix):
    tr, n = x_ref.shape
    nblk = n // SCAN_BLOCK
    tri = tri_ref[...]
    carry = jnp.zeros((tr, 1), F32)
    order = range(nblk - 1, -1, -1) if suffix else range(nblk)
    for blk in order:
        cols = slice(blk * SCAN_BLOCK, (blk + 1) * SCAN_BLOCK)
        x = x_ref[:, cols]
        r = _dot(jnp.concatenate(_split3(x), axis=0), tri)
        y = r[0:tr] + r[tr:2 * tr] + r[2 * tr:3 * tr] + carry
        if suffix:
            o_ref[:, cols] = y
            carry = y[:, 0:1] + x[:, 0:1]
        else:
            o_ref[:, cols] = -y
            carry = y[:, SCAN_BLOCK - 1:SCAN_BLOCK]


def _scan_lanes(x, suffix):
    rows, n = x.shape
    tr = min(SCAN_ROWS, rows)
    i = lax.broadcasted_iota(jnp.int32, (SCAN_BLOCK, SCAN_BLOCK), 0)
    j = lax.broadcasted_iota(jnp.int32, (SCAN_BLOCK, SCAN_BLOCK), 1)
    tri = jnp.where(i > j if suffix else i <= j, 1.0, 0.0).astype(BF16)
    blk = pl.BlockSpec((tr, n), lambda i: (i, 0))
    return pl.pallas_call(
        functools.partial(_scan_kernel, suffix=suffix),
        grid=(rows // tr,),
        in_specs=[blk, _const_spec(tri.shape)],
        out_specs=blk,
        out_shape=jax.ShapeDtypeStruct((rows, n), F32),
        compiler_params=_cparams("arbitrary"),
        name="logf_suffix" if suffix else "logf_prefix",
    )(x, tri)


def _fox_prompt_kernel(q_ref, k_ref, v_ref, b_ref, o_ref, qs_ref, m_ref, l_ref, acc_ref, *, group, hd):
    i = pl.program_id(2)
    t = q_ref.shape[0]
    rows = group * t
    for g in range(group):
        qs_ref[g * t:(g + 1) * t, :] = q_ref[:, g * hd:(g + 1) * hd]
    m_ref[...] = jnp.full(m_ref.shape, NEG, F32)
    l_ref[...] = jnp.zeros(l_ref.shape, F32)
    acc_ref[...] = jnp.zeros(acc_ref.shape, F32)

    def tile(kt, masked):
        k0 = pl.multiple_of(kt * t, t)
        s = _dot_nt(qs_ref[...], k_ref[pl.ds(k0, t), :]).reshape(group, t, t)
        s = s + (b_ref[0, 0, :, pl.ds(k0, t)] * LOG2E)[:, None, :]
        if masked:
            r = lax.broadcasted_iota(jnp.int32, (group, t, t), 1)
            c = lax.broadcasted_iota(jnp.int32, (group, t, t), 2)
            s = jnp.where(c <= r, s, NEG)
        s = s.reshape(rows, t)
        m_old = m_ref[...]
        m_new = jnp.maximum(m_old, jnp.max(s, axis=-1, keepdims=True))
        p = jnp.exp2(s - jnp.concatenate([m_new] * (t // LANES), axis=1))
        alpha = jnp.exp2(m_old - m_new)
        l_ref[...] = alpha * l_ref[...] + jnp.sum(p, axis=-1, keepdims=True)
        acc_ref[...] = alpha * acc_ref[...] + _dot(p.astype(BF16), v_ref[pl.ds(k0, t), :])
        m_ref[...] = m_new

    def body(kt, carry):
        tile(kt, False)
        return carry

    lax.fori_loop(0, i, body, 0)
    tile(i, True)
    o = acc_ref[...] / l_ref[...]
    for g in range(group):
        o_ref[:, g * hd:(g + 1) * hd] = o[g * t:(g + 1) * t].astype(o_ref.dtype)


def _fox_prompt_attn(q, kb, vb, bias, batch, seq, nkv, hd):
    assert hd == LANES
    group = q.shape[1] // (nkv * hd)
    t = FOX_TQ
    nq = seq // t
    return pl.pallas_call(
        functools.partial(_fox_prompt_kernel, group=group, hd=hd),
        grid=(batch, nkv, nq),
        in_specs=[pl.BlockSpec((t, group * hd), lambda b, h, i: (b * nq + i, h)),
                  pl.BlockSpec((seq, hd), lambda b, h, i: (b, h)),
                  pl.BlockSpec((seq, hd), lambda b, h, i: (b, h)),
                  pl.BlockSpec((1, 1, group, seq), lambda b, h, i: (b, h, 0, 0))],
        out_specs=pl.BlockSpec((t, group * hd), lambda b, h, i: (b * nq + i, h)),
        out_shape=jax.ShapeDtypeStruct(q.shape, BF16),
        scratch_shapes=[pltpu.VMEM((group * t, hd), BF16), pltpu.VMEM((group * t, LANES), F32),
                        pltpu.VMEM((group * t, LANES), F32), pltpu.VMEM((group * t, hd), F32)],
        compiler_params=_cparams("arbitrary", "arbitrary", "arbitrary"),
        name="fox_prompt_attn",
    )(q, kb, vb, bias)


def _fox_sample_kernel(pt_ref, q_ref, kn_ref, vn_ref, bn_ref, e_ref, tri_ref, kpool, vpool, lpool,
                       o_ref, kbuf, vbuf, lbuf, sem, *, nkv, hd, n_chunks, rows_q, group):
    b = pl.program_id(0)
    nb = pl.num_programs(0)
    page_rows = PAGE_SIZE * nkv
    lc = PAGES_PER_CHUNK * PAGE_SIZE
    nh = lbuf.shape[2]
    pages_per_block = SCAN_BLOCK // PAGE_SIZE
    nblk = lc // SCAN_BLOCK

    def start_chunk(bb, step, slot):
        c = n_chunks - 1 - step
        for p in range(PAGES_PER_CHUNK):
            page = pt_ref[bb, c * PAGES_PER_CHUNK + p]
            src = pl.ds(pl.multiple_of(page * page_rows, page_rows), page_rows)
            dst = pl.ds(p * page_rows, page_rows)
            pltpu.make_async_copy(kpool.at[src, :], kbuf.at[slot, dst, :], sem.at[0, slot]).start()
            pltpu.make_async_copy(vpool.at[src, :], vbuf.at[slot, dst, :], sem.at[1, slot]).start()
            pltpu.make_async_copy(lpool.at[page], lbuf.at[slot, p], sem.at[2, slot]).start()

    def wait_chunk(slot):
        for buf, s in ((kbuf, 0), (vbuf, 1), (lbuf, 2)):
            pltpu.make_async_copy(buf.at[slot], buf.at[slot], sem.at[s, slot]).wait()

    ahead = DECODE_SLOTS - 1

    @pl.when(b == 0)
    def _():
        for g in range(ahead):
            start_chunk(g // n_chunks, g % n_chunks, g % DECODE_SLOTS)

    e = e_ref[...]
    tri = tri_ref[...]
    q = [q_ref[0, h] for h in range(nkv)]

    def chunk(step, carry):
        g = b * n_chunks + step
        slot = lax.rem(g, DECODE_SLOTS)
        nxt = g + ahead

        @pl.when(nxt < nb * n_chunks)
        def _():
            start_chunk(lax.div(nxt, n_chunks), lax.rem(nxt, n_chunks), lax.rem(nxt, DECODE_SLOTS))

        wait_chunk(slot)

        xs = [jnp.concatenate([lbuf[slot, k * pages_per_block + p] for p in range(pages_per_block)], axis=1)
              for k in range(nblk)]
        r = _dot(jnp.concatenate([part for x in xs for part in _split3(x)], axis=0), tri)
        sfx_carry = carry[nkv]
        ys = [None] * nblk
        for k in range(nblk - 1, -1, -1):
            local = r[(3 * k) * nh:(3 * k + 1) * nh] + r[(3 * k + 1) * nh:(3 * k + 2) * nh] \
                + r[(3 * k + 2) * nh:(3 * k + 3) * nh]
            ys[k] = local + sfx_carry
            sfx_carry = sfx_carry + (local[:, 0:1] + xs[k][:, 0:1])
        bias = _dot(e, jnp.concatenate(_split3(jnp.concatenate(ys, axis=1) * LOG2E), axis=0))
        new = []
        for h in range(nkv):
            m_old, l_old, acc_old = carry[h]
            k = kbuf[slot, pl.ds(h, lc, stride=nkv), :].astype(BF16)
            v = vbuf[slot, pl.ds(h, lc, stride=nkv), :].astype(BF16)
            s = _dot_nt(q[h], k) + bias[h * rows_q:(h + 1) * rows_q]
            m_new = jnp.maximum(m_old, jnp.max(s, axis=-1, keepdims=True))
            p = jnp.exp2(s - m_new)
            alpha = jnp.exp2(m_old - m_new)
            new.append((m_new, alpha * l_old + jnp.sum(p, axis=-1, keepdims=True),
                        alpha * acc_old + _dot(p.astype(BF16), v)))
        return tuple(new) + (sfx_carry,)

    init = tuple((jnp.full((rows_q, 1), NEG, F32), jnp.zeros((rows_q, 1), F32),
                  jnp.zeros((rows_q, hd), F32)) for _ in range(nkv)) + (jnp.zeros((nh, 1), F32),)
    state = lax.fori_loop(0, n_chunks, chunk, init)

    tn = kn_ref.shape[1]
    bias_n = _dot_nt(e, bn_ref[0])
    r = lax.broadcasted_iota(jnp.int32, (rows_q, tn), 0)
    c = lax.broadcasted_iota(jnp.int32, (rows_q, tn), 1)
    ok = c * group <= r
    for h in range(nkv):
        m_old, l_old, acc_old = state[h]
        k = kn_ref[0, :, h * hd:(h + 1) * hd].astype(BF16)
        v = vn_ref[0, :, h * hd:(h + 1) * hd].astype(BF16)
        s = jnp.where(ok, _dot_nt(q[h], k) + bias_n[h * rows_q:(h + 1) * rows_q], NEG)
        m_new = jnp.maximum(m_old, jnp.max(s, axis=-1, keepdims=True))
        p = jnp.exp2(s - m_new)
        alpha = jnp.exp2(m_old - m_new)
        l_new = alpha * l_old + jnp.sum(p, axis=-1, keepdims=True)
        acc = alpha * acc_old + _dot(p.astype(BF16), v)
        o_ref[0, h] = (acc / l_new).astype(o_ref.dtype)


def _fox_sample_attn(page_table, q, k_new, v_new, bias_new3, emat, kpool, vpool, lpool, nkv, hd, group):
    nb, _, rows_q, _ = q.shape
    n_pages = page_table.shape[1]
    nh = lpool.shape[1]
    n_chunks = n_pages // PAGES_PER_CHUNK
    buf_rows = PAGES_PER_CHUNK * PAGE_SIZE * nkv
    tn = k_new.shape[1]
    i = lax.broadcasted_iota(jnp.int32, (SCAN_BLOCK, SCAN_BLOCK), 0)
    j = lax.broadcasted_iota(jnp.int32, (SCAN_BLOCK, SCAN_BLOCK), 1)
    tri = jnp.where(i > j, 1.0, 0.0).astype(BF16)
    const = lambda a: pl.BlockSpec(a.shape, lambda b, pt: (0,) * a.ndim)
    grid_spec = pltpu.PrefetchScalarGridSpec(
        num_scalar_prefetch=1,
        grid=(nb,),
        in_specs=[pl.BlockSpec((1, nkv, rows_q, hd), lambda b, pt: (b, 0, 0, 0)),
                  pl.BlockSpec((1, tn, nkv * hd), lambda b, pt: (b, 0, 0)),
                  pl.BlockSpec((1, tn, nkv * hd), lambda b, pt: (b, 0, 0)),
                  pl.BlockSpec((1,) + bias_new3.shape[1:], lambda b, pt: (b, 0, 0)),
                  const(emat), const(tri),
                  pl.BlockSpec(memory_space=pl.ANY),
                  pl.BlockSpec(memory_space=pl.ANY),
                  pl.BlockSpec(memory_space=pl.ANY)],
        out_specs=pl.BlockSpec((1, nkv, rows_q, hd), lambda b, pt: (b, 0, 0, 0)),
        scratch_shapes=[pltpu.VMEM((DECODE_SLOTS, buf_rows, hd), F32), pltpu.VMEM((DECODE_SLOTS, buf_rows, hd), F32),
                        pltpu.VMEM((DECODE_SLOTS, PAGES_PER_CHUNK, nh, PAGE_SIZE), F32),
                        pltpu.SemaphoreType.DMA((3, DECODE_SLOTS))],
    )
    return pl.pallas_call(
        functools.partial(_fox_sample_kernel, nkv=nkv, hd=hd, n_chunks=n_chunks, rows_q=rows_q, group=group),
        grid_spec=grid_spec,
        out_shape=jax.ShapeDtypeStruct(q.shape, BF16),
        compiler_params=_cparams("arbitrary"),
        name="fox_sample_attn",
    )(page_table, q, k_new, v_new, bias_new3, emat, tri, kpool, vpool, lpool)


def _out_proj_kernel(o_ref, h_ref, w_ref, g_ref, h1_ref, c_ref):
    o = o_ref[...]
    tn = 512
    for c in range(h_ref.shape[1] // tn):
        cs = slice(c * tn, (c + 1) * tn)
        h1_ref[:, cs] = h_ref[:, cs] + _dot(o, w_ref[:, cs].astype(BF16))
    c_ref[...] = (_rms(h1_ref[...]) * g_ref[...]).astype(BF16)


def _out_proj(o, h, w, g):
    m, d = h.shape
    tm = min(ROW_TILE, m)
    row = lambda n: pl.BlockSpec((tm, n), lambda i: (i, 0))
    return pl.pallas_call(
        _out_proj_kernel,
        grid=(m // tm,),
        in_specs=[row(o.shape[1]), row(d), _weight_spec(w), _const_spec(g.shape)],
        out_specs=[row(d), row(d)],
        out_shape=[jax.ShapeDtypeStruct((m, d), F32), jax.ShapeDtypeStruct((m, d), BF16)],
        compiler_params=_cparams("arbitrary"),
        name="attn_out_proj",
    )(o, h, w, g)


def _ffn_begin(j, h_ref, h2_ref):
    @pl.when(j == 0)
    def _():
        h2_ref[...] = h_ref[...]


def _ffn_epilogue(j, gn_ref, h2_ref, n_ref):
    @pl.when(j == pl.num_programs(1) - 1)
    def _():
        n_ref[...] = (_rms(h2_ref[...]) * gn_ref[...]).astype(BF16)


def _ffn_prompt_kernel(c_ref, wg_ref, wv_ref, cwg_ref, cwv_ref, cbg_ref, cbv_ref, wo_ref, h_ref, gn_ref,
                       h2_ref, n_ref, tg_ref, tv_ref, ug_ref, uv_ref, pg_ref, pv_ref, *, tiles_per_seq):
    i, j = pl.program_id(0), pl.program_id(1)
    tm = c_ref.shape[0]
    first = lax.rem(i, tiles_per_seq) == 0
    _ffn_begin(j, h_ref, h2_ref)

    @pl.when((i == 0) & (j == 0))
    def _():
        pg_ref[...] = jnp.zeros(pg_ref.shape, F32)
        pv_ref[...] = jnp.zeros(pv_ref.shape, F32)

    x = c_ref[...]
    for u_ref, p_ref, w_ref, t_ref in ((ug_ref, pg_ref, wg_ref, tg_ref), (uv_ref, pv_ref, wv_ref, tv_ref)):
        u_ref[0:HALO, :] = jnp.where(first, 0.0, p_ref[j])
        u_ref[HALO:, :] = _dot(x, w_ref[...])
        tail = u_ref[tm:tm + HALO, :]
        p_ref[j] = tail
        t_ref[...] = tail

    def conv(u_ref, cw_ref, cb_ref):
        y = cb_ref[...]
        for k in range(CONV_W):
            y = y + cw_ref[k:k + 1, :] * u_ref[pl.ds(HALO - (CONV_W - 1) + k, tm), :]
        return y

    act = (jax.nn.gelu(conv(ug_ref, cwg_ref, cbg_ref), approximate=True) * conv(uv_ref, cwv_ref, cbv_ref)).astype(BF16)
    h2_ref[...] += _dot(act, wo_ref[...])
    _ffn_epilogue(j, gn_ref, h2_ref, n_ref)


def _ffn_sample_kernel(c_ref, pg_ref, pv_ref, wg_ref, wv_ref, cwg_ref, cwv_ref, cbg_ref, cbv_ref, wo_ref,
                       h_ref, gn_ref, h2_ref, n_ref, tg_ref, tv_ref):
    j = pl.program_id(1)
    nb = pg_ref.shape[1]
    nt = c_ref.shape[0] // nb
    _ffn_begin(j, h_ref, h2_ref)
    x = c_ref[...]

    def conv(u, p_ref, cw_ref, cb_ref, t_ref):
        slabs = [p_ref[k] for k in range(CONV_W - 1)] + [u[t * nb:(t + 1) * nb] for t in range(nt)]
        for k in range(CONV_W - 1):
            t_ref[k] = slabs[nt + k]
        y = cb_ref[...]
        for k in range(CONV_W):
            y = y + cw_ref[k:k + 1, :] * jnp.concatenate(slabs[k:k + nt], axis=0)
        return y

    g = conv(_dot(x, wg_ref[...]), pg_ref, cwg_ref, cbg_ref, tg_ref)
    v = conv(_dot(x, wv_ref[...]), pv_ref, cwv_ref, cbv_ref, tv_ref)
    act = (jax.nn.gelu(g, approximate=True) * v).astype(BF16)
    h2_ref[...] += _dot(act, wo_ref[...])
    _ffn_epilogue(j, gn_ref, h2_ref, n_ref)


def _conv_ffn(c, h, layer, w_in, conv_w, conv_b, w_out, g_next, past, tiles_per_seq):
    m, d = h.shape
    dff = w_out.shape[1]
    tf = FFN_TILE
    nf = dff // tf
    tm = min(FFN_ROWS, m)
    wspecs = [pl.BlockSpec((None, d, tf), lambda i, j: (layer, 0, j)),
              pl.BlockSpec((None, d, tf), lambda i, j: (layer, 0, nf + j)),
              pl.BlockSpec((None, CONV_W, tf), lambda i, j: (layer, 0, j)),
              pl.BlockSpec((None, CONV_W, tf), lambda i, j: (layer, 0, nf + j)),
              pl.BlockSpec((None, 1, tf), lambda i, j: (layer, 0, j)),
              pl.BlockSpec((None, 1, tf), lambda i, j: (layer, 0, nf + j)),
              pl.BlockSpec((None, tf, d), lambda i, j: (layer, j, 0))]
    wargs = [w_in, w_in, conv_w, conv_w, conv_b, conv_b, w_out]
    row = pl.BlockSpec((tm, d), lambda i, j: (i, 0))
    tail_specs = [pl.BlockSpec((tm, d), lambda i, j: (i, 0), pipeline_mode=pl.Buffered(1)),
                  pl.BlockSpec((1, d), lambda i, j: (0, 0))]
    if past is None:
        head_specs = [row]
        head_args = [c]
        kern = functools.partial(_ffn_prompt_kernel, tiles_per_seq=tiles_per_seq)
        scratch = [pltpu.VMEM((tm + HALO, tf), F32)] * 2 + [pltpu.VMEM((nf, HALO, tf), F32)] * 2
        state = jax.ShapeDtypeStruct((m // tm, HALO, dff), F32)
        state_spec = pl.BlockSpec((None, HALO, tf), lambda i, j: (i, 0, j))
    else:
        scratch = []
        nb = past.shape[2]
        head_specs = [row, pl.BlockSpec((None, CONV_W - 1, nb, tf), lambda i, j: (layer, 0, 0, j)),
                      pl.BlockSpec((None, CONV_W - 1, nb, tf), lambda i, j: (layer, 0, 0, nf + j))]
        head_args = [c, past, past]
        kern = _ffn_sample_kernel
        state = jax.ShapeDtypeStruct((CONV_W - 1, nb, dff), F32)
        state_spec = pl.BlockSpec((CONV_W - 1, nb, tf), lambda i, j: (0, 0, j))
    return pl.pallas_call(
        kern,
        grid=(m // tm, nf),
        in_specs=head_specs + wspecs + tail_specs,
        out_specs=[row, row, state_spec, state_spec],
        out_shape=[jax.ShapeDtypeStruct((m, d), F32), jax.ShapeDtypeStruct((m, d), BF16), state, state],
        scratch_shapes=scratch,
        compiler_params=_cparams("arbitrary", "arbitrary"),
        name="conv_ffn_prompt" if past is None else "conv_ffn_sample",
    )(*head_args, *wargs, h, g_next)


def _ple_kernel(h_ref, n_ref, p_ref, wp_ref, wg_ref, *rest, n_norms):
    g_refs = rest[:n_norms]
    h3_ref = rest[n_norms]
    a_refs = rest[n_norms + 1:]
    n = n_ref[...]
    p = p_ref[...].astype(BF16)
    d = h_ref.shape[1]
    tn = 512
    for c in range(d // tn):
        cs = slice(c * tn, (c + 1) * tn)
        gate = jax.nn.sigmoid(_dot(n, wg_ref[:, cs].astype(BF16)))
        h3_ref[:, cs] = h_ref[:, cs] + _dot(p, wp_ref[:, cs].astype(BF16)) * gate
    if n_norms:
        xhat = _rms(h3_ref[...])
        for g_ref, a_ref in zip(g_refs, a_refs):
            a_ref[...] = (xhat * g_ref[...]).astype(BF16)


def _ple(h, n, p, layer, wp, wg, gains):
    m, d = h.shape
    tm = min(ROW_TILE, m)
    row = lambda w: pl.BlockSpec((tm, w), lambda i: (i, 0))
    k = len(gains)
    outs = pl.pallas_call(
        functools.partial(_ple_kernel, n_norms=k),
        grid=(m // tm,),
        in_specs=[row(d), row(d), pl.BlockSpec((None, tm, p.shape[2]), lambda i: (layer, i, 0)),
                  _weight_spec(wp, layer), _weight_spec(wg, layer)]
                 + [_const_spec(g.shape) for g in gains],
        out_specs=[row(d)] * (k + 1),
        out_shape=[jax.ShapeDtypeStruct((m, d), F32)] + [jax.ShapeDtypeStruct((m, d), BF16)] * k,
        compiler_params=_cparams("arbitrary"),
        name="ple",
    )(h, n, p, wp, wg, *gains)
    return outs


def _swa_proj_kernel(a_ref, akv_ref, wq_ref, wkv_ref, qg_ref, kg_ref, q_ref, k_ref, v_ref, *, nkv, hd):
    a = a_ref[...]
    qg = qg_ref[...]
    lane = lax.broadcasted_iota(jnp.int32, (a.shape[0], LANES), 1)
    low = lane < hd
    for c in range(wq_ref.shape[1] // MXU_N):
        z = _dot(a, wq_ref[:, c * MXU_N:(c + 1) * MXU_N].astype(BF16))
        for s in range(MXU_N // LANES):
            zz = z[:, s * LANES:(s + 1) * LANES]
            sq = zz * zz
            s_lo = jnp.sum(jnp.where(low, sq, 0.0), axis=-1, keepdims=True)
            s_hi = jnp.sum(jnp.where(low, 0.0, sq), axis=-1, keepdims=True)
            inv = jnp.where(low, lax.rsqrt(s_lo / hd + EPS), lax.rsqrt(s_hi / hd + EPS))
            lo = c * MXU_N + s * LANES
            q_ref[:, lo:lo + LANES] = (zz * inv * qg).astype(BF16)
    akv = akv_ref[...]
    kg = kg_ref[...]
    for c in range(nkv // MXU_N):
        z = _dot(akv, wkv_ref[:, c * MXU_N:(c + 1) * MXU_N])
        for s in range(MXU_N // LANES):
            lo = c * MXU_N + s * LANES
            k_ref[:, lo:lo + LANES] = _rms(z[:, s * LANES:(s + 1) * LANES]) * kg
    for c in range(nkv // MXU_N):
        v_ref[:, c * MXU_N:(c + 1) * MXU_N] = _dot(akv, wkv_ref[:, nkv + c * MXU_N:nkv + (c + 1) * MXU_N])


def _swa_proj(a, akv, wq, wkv_dup, qg, kg, hd):
    m, d = a.shape
    nkv = wkv_dup.shape[2] // 2
    nq = wq.shape[2]
    tm = min(ROW_TILE, m)
    row = lambda n: pl.BlockSpec((tm, n), lambda i: (i, 0))
    return pl.pallas_call(
        functools.partial(_swa_proj_kernel, nkv=nkv, hd=hd),
        grid=(m // tm,),
        in_specs=[row(d), row(d), _weight_spec(wq), _weight_spec(wkv_dup),
                  _const_spec(qg.shape), _const_spec(kg.shape)],
        out_specs=[row(nq), row(nkv), row(nkv)],
        out_shape=[jax.ShapeDtypeStruct((m, nq), BF16), jax.ShapeDtypeStruct((m, nkv), F32),
                   jax.ShapeDtypeStruct((m, nkv), F32)],
        compiler_params=_cparams("arbitrary"),
        name="swa_proj",
    )(a, akv, wq, wkv_dup, qg, kg)


def _swa_prompt_kernel(slope_ref, sink_ref, q_ref, kp_ref, kc_ref, vp_ref, vc_ref, o_ref, *, nkv, hd):
    i = pl.program_id(1)
    w = q_ref.shape[0]
    pairs = q_ref.shape[1] // (nkv * LANES)
    lane = lax.broadcasted_iota(jnp.int32, (w, LANES), 1)
    low = lane < hd
    r = lax.broadcasted_iota(jnp.int32, (w, 2 * w), 0)
    c = lax.broadcasted_iota(jnp.int32, (w, 2 * w), 1)
    dist = r + w - c
    valid = (dist >= 0) & (dist < w) & ((c >= w) | (i > 0))
    base = jnp.where(valid, dist.astype(F32), MASK_DIST)
    top = lax.broadcasted_iota(jnp.int32, (2 * w, 1), 0) < w
    for h in range(nkv):
        k = jnp.concatenate([kp_ref[:, h * LANES:(h + 1) * LANES], kc_ref[:, h * LANES:(h + 1) * LANES]],
                            axis=0).astype(BF16)
        v = jnp.concatenate([vp_ref[:, h * LANES:(h + 1) * LANES], vc_ref[:, h * LANES:(h + 1) * LANES]],
                            axis=0).astype(BF16)
        for pr in range(pairs):
            col = (h * pairs + pr) * LANES
            h0 = 2 * (h * pairs + pr)
            qp = q_ref[:, col:col + LANES]
            rows = jnp.concatenate([jnp.where(low, qp, jnp.zeros_like(qp)),
                                    jnp.where(low, jnp.zeros_like(qp), qp)], axis=0)
            s = _dot_nt(rows, k) - jnp.concatenate([slope_ref[h0] * base, slope_ref[h0 + 1] * base], axis=0)
            sink = jnp.where(top, sink_ref[h0], sink_ref[h0 + 1])
            m = jnp.maximum(jnp.max(s, axis=-1, keepdims=True), sink)
            p = jnp.exp2(s - m)
            den = jnp.sum(p, axis=-1, keepdims=True) + jnp.exp2(sink - m)
            o = _dot(p.astype(BF16), v) / den
            o_ref[:, col:col + LANES] = jnp.where(low, o[:w], o[w:]).astype(o_ref.dtype)


def _swa_prompt_attn(q, kdup, vdup, slopes, sinks, batch, seq, nkv, hd):
    nblk = seq // WINDOW
    dq = q.shape[1]
    dk = kdup.shape[1]
    cur = lambda b, i: (b * nblk + i, 0)
    prev = lambda b, i: (b * nblk + jnp.maximum(i - 1, 0), 0)
    smem = pl.BlockSpec(memory_space=pltpu.SMEM)
    return pl.pallas_call(
        functools.partial(_swa_prompt_kernel, nkv=nkv, hd=hd),
        grid=(batch, nblk),
        in_specs=[smem, smem, pl.BlockSpec((WINDOW, dq), cur), pl.BlockSpec((WINDOW, dk), prev),
                  pl.BlockSpec((WINDOW, dk), cur), pl.BlockSpec((WINDOW, dk), prev),
                  pl.BlockSpec((WINDOW, dk), cur)],
        out_specs=pl.BlockSpec((WINDOW, dq), cur),
        out_shape=jax.ShapeDtypeStruct(q.shape, BF16),
        compiler_params=_cparams("arbitrary", "arbitrary"),
        name="swa_prompt_attn",
    )(slopes, sinks, q, kdup, kdup, vdup, vdup)


def _swa_sample_kernel(q_ref, kc_ref, vc_ref, kn_ref, vn_ref, slope_ref, sink_ref, o_ref, *, heads):
    rows = q_ref.shape[1]
    w = kc_ref.shape[2]
    tn = kn_ref.shape[1]
    slope = slope_ref[...]
    sink = sink_ref[...]
    t_p = lax.broadcasted_iota(jnp.int32, (rows, w), 0) // heads
    j_p = lax.broadcasted_iota(jnp.int32, (rows, w), 1)
    dist_p = t_p + w - j_p
    bias_p = jnp.where(dist_p < WINDOW, slope * dist_p.astype(F32), -NEG)
    t_n = lax.broadcasted_iota(jnp.int32, (rows, tn), 0) // heads
    j_n = lax.broadcasted_iota(jnp.int32, (rows, tn), 1)
    dist_n = t_n - j_n
    bias_n = jnp.where(dist_n >= 0, slope * dist_n.astype(F32), -NEG)
    for b in range(q_ref.shape[0]):
        q = q_ref[b]
        s_p = _dot(q, kc_ref[b].astype(BF16)) - bias_p
        s_n = _dot_nt(q, kn_ref[b].astype(BF16)) - bias_n
        m = jnp.maximum(jnp.maximum(jnp.max(s_p, axis=-1, keepdims=True), jnp.max(s_n, axis=-1, keepdims=True)),
                        sink)
        p_p = jnp.exp2(s_p - m)
        p_n = jnp.exp2(s_n - m)
        den = jnp.sum(p_p, axis=-1, keepdims=True) + jnp.sum(p_n, axis=-1, keepdims=True) + jnp.exp2(sink - m)
        o = _dot_nt(p_p.astype(BF16), vc_ref[b].astype(BF16)) + _dot(p_n.astype(BF16), vn_ref[b].astype(BF16))
        o_ref[b] = o / den


def _swa_sample_attn(q, kct, vct, kn, vn, slope_col, sink_col, heads):
    nb, rows, dk = q.shape
    g = SWA_SAMPLE_ROWS
    blk = lambda a: pl.BlockSpec((g,) + a.shape[1:], lambda b: (b, 0, 0))
    return pl.pallas_call(
        functools.partial(_swa_sample_kernel, heads=heads),
        grid=(nb // g,),
        in_specs=[blk(q), blk(kct), blk(vct), blk(kn), blk(vn), _const_spec(slope_col.shape),
                  _const_spec(sink_col.shape)],
        out_specs=blk(q),
        out_shape=jax.ShapeDtypeStruct((nb, rows, dk), F32),
        compiler_params=_cparams("arbitrary"),
        name="swa_sample_attn",
    )(q, kct, vct, kn, vn, slope_col, sink_col)


def kernel(x_prompt, x_sample, cache_fox_k, cache_fox_v, cache_fox_logf, cache_win_k, cache_win_v, state_conv, page_table, p_prompt, p_sample, norm_attn_g, norm_ffn_g, norm_ple_g, fox_w_in, fox_b_f, fox_q_norm_g, fox_k_norm_g, fox_w_out, kv_norm_g, swa_w_kv, swa_k_norm_g, swa_w_q, swa_q_norm_g, swa_sinks, swa_w_out, ffn_w_in, ffn_conv_w, ffn_conv_b, ffn_w_out, ple_w_proj, ple_w_gate):
    B, S, D = x_prompt.shape
    NB, T, _ = x_sample.shape
    HD_A = fox_q_norm_g.shape[-1]
    H_A = fox_b_f.shape[-1]
    KVH_A = cache_fox_k.shape[3]
    G_A = H_A // KVH_A
    HD_B = swa_q_norm_g.shape[-1]
    H_B = swa_sinks.shape[-1]
    KVH_B = cache_win_k.shape[2]
    G_B = H_B // KVH_B
    WIN = cache_win_k.shape[1]
    L = page_table.shape[1] * PAGE_SIZE
    DFF = ffn_w_out.shape[1]
    NQ_A, NK_A = H_A * HD_A, KVH_A * HD_A
    row2 = lambda g: g.reshape(1, -1)

    bf_a = fox_b_f
    qg_a = row2(fox_q_norm_g[0] * (HD_A ** -0.5 * LOG2E))
    kg_a = row2(fox_k_norm_g[0])
    dup = lambda w: jnp.repeat(w.reshape(D, KVH_B, 1, HD_B), LANES // HD_B, axis=2).reshape(D, KVH_B * LANES)
    wk_b, wv_b = jnp.split(swa_w_kv, 2, axis=-1)
    wkv_b = jnp.concatenate([dup(wk_b), dup(wv_b)], axis=1).astype(BF16)[None]
    qg_b = row2(jnp.tile(swa_q_norm_g[0] * (HD_B ** -0.5 * LOG2E), LANES // HD_B))
    kg_b = row2(jnp.tile(swa_k_norm_g, LANES // HD_B))
    ffn_wi = ffn_w_in.astype(BF16)
    ffn_wo = ffn_w_out.astype(BF16)
    slopes = jnp.exp2(-8.0 * jnp.arange(1, H_B + 1, dtype=F32) / H_B) * LOG2E
    sinks = swa_sinks[0] * LOG2E

    xp = x_prompt.reshape(B * S, D)
    xs = x_sample.transpose(1, 0, 2).reshape(T * NB, D)
    pp = p_prompt.reshape(2, B * S, -1)
    ps = p_sample.transpose(0, 2, 1, 3).reshape(2, T * NB, -1)
    past = state_conv.transpose(0, 2, 1, 3)

    g0 = row2(norm_attn_g[0])
    w_in_t = fox_w_in.transpose(0, 2, 1)
    q_p, k_p, v_p, kb_p, vb_p, lf_p = _fox_proj(xp, g0, w_in_t, bf_a, qg_a, kg_a, NQ_A, NK_A)
    q_s, k_s, v_s, _, _, lf_s = _fox_proj(xs, g0, w_in_t, bf_a, qg_a, kg_a, NQ_A, NK_A)

    bias_p = _scan_lanes(lf_p.reshape(B, S, H_A).transpose(0, 2, 1).reshape(B * H_A, S), suffix=False)
    o_p = _fox_prompt_attn(q_p, kb_p, vb_p, bias_p.reshape(B, KVH_A, G_A, S), B, S, KVH_A, HD_A)

    tn = 16
    lf_new = lf_s.reshape(T, NB, H_A).transpose(1, 2, 0).reshape(NB * H_A, T)
    bias_new = _scan_lanes(jnp.pad(lf_new, ((0, 0), (0, SCAN_BLOCK - T))), suffix=False)[:, :tn]
    bias_new3 = jnp.concatenate(_split3(bias_new.reshape(NB, H_A, tn).transpose(0, 2, 1) * LOG2E), axis=-1)
    rows_q = T * G_A
    head_of_row = (jnp.arange(KVH_A)[:, None] * G_A + jnp.arange(rows_q)[None, :] % G_A).reshape(-1)
    emat = jnp.tile(jax.nn.one_hot(head_of_row, H_A, dtype=BF16), (1, 3))
    qs_b = q_s.reshape(T, NB, KVH_A, G_A, HD_A).transpose(1, 2, 0, 3, 4).reshape(NB, KVH_A, rows_q, HD_A)
    pad_new = lambda a: jnp.pad(a.reshape(T, NB, NK_A).transpose(1, 0, 2), ((0, 0), (0, tn - T), (0, 0)))
    kpool = cache_fox_k.reshape(-1, HD_A)
    vpool = cache_fox_v.reshape(-1, HD_A)
    o_s = _fox_sample_attn(page_table, qs_b, pad_new(k_s), pad_new(v_s), bias_new3, emat, kpool, vpool,
                           cache_fox_logf.reshape(cache_fox_logf.shape[1:]).transpose(0, 2, 1), KVH_A, HD_A, G_A)
    o_s = o_s.reshape(NB, KVH_A, T, G_A, HD_A).transpose(2, 0, 1, 3, 4).reshape(T * NB, NQ_A)

    gf0 = row2(norm_ffn_g[0])
    h_p, c_p = _out_proj(o_p, xp, fox_w_out, gf0)
    h_s, c_s = _out_proj(o_s, xs, fox_w_out, gf0)

    gp0 = row2(norm_ple_g[0])
    tiles_per_seq = S // FFN_ROWS
    conv_b = ffn_conv_b[:, None, :]
    h_p, n_p, *state_p0 = _conv_ffn(c_p, h_p, 0, ffn_wi, ffn_conv_w, conv_b, ffn_wo, gp0, None, tiles_per_seq)
    h_s, n_s, *state_s0 = _conv_ffn(c_s, h_s, 0, ffn_wi, ffn_conv_w, conv_b, ffn_wo, gp0, past, 1)

    gq1, gkv = row2(norm_attn_g[1]), row2(kv_norm_g)
    h_p, a_p, akv_p = _ple(h_p, n_p, pp, 0, ple_w_proj, ple_w_gate, [gq1, gkv])
    h_s, a_s, akv_s = _ple(h_s, n_s, ps, 0, ple_w_proj, ple_w_gate, [gq1, gkv])

    q1_p, kd_p, vd_p = _swa_proj(a_p, akv_p, swa_w_q, wkv_b, qg_b, kg_b, HD_B)
    q1_s, kd_s, vd_s = _swa_proj(a_s, akv_s, swa_w_q, wkv_b, qg_b, kg_b, HD_B)
    undup = lambda a: a.reshape(a.shape[0], KVH_B, LANES // HD_B, HD_B)[:, :, 0]
    o1_p = _swa_prompt_attn(q1_p, kd_p, vd_p, slopes, sinks, B, S, KVH_B, HD_B)

    ksh_s = undup(kd_s).reshape(T, NB, KVH_B * HD_B).transpose(1, 0, 2)
    vsh_s = undup(vd_s).reshape(T, NB, KVH_B * HD_B).transpose(1, 0, 2)
    q1h = q1_s.reshape(T, NB, H_B, 1, HD_B).transpose(1, 0, 2, 3, 4)
    slab = (jnp.arange(H_B)[:, None] // G_B == jnp.arange(KVH_B)[None, :])[None, None, :, :, None]
    q1x = jnp.where(slab, q1h, jnp.zeros((), BF16)).reshape(NB, T * H_B, KVH_B * HD_B)
    padn = lambda a: jnp.pad(a, ((0, 0), (0, tn - T), (0, 0)))
    slope_col = jnp.tile(slopes, T).reshape(T * H_B, 1)
    sink_col = jnp.tile(sinks, T).reshape(T * H_B, 1)
    feature_major = lambda a: a.transpose(0, 2, 3, 1).reshape(NB, KVH_B * HD_B, WIN)
    o1x = _swa_sample_attn(q1x, feature_major(cache_win_k), feature_major(cache_win_v),
                           padn(ksh_s), padn(vsh_s), slope_col, sink_col, H_B)
    o1x = o1x.reshape(NB, T, H_B, KVH_B, HD_B)
    o1_s = jnp.sum(jnp.where(slab, o1x, 0.0), axis=3)
    o1_s = o1_s.transpose(1, 0, 2, 3).reshape(T * NB, H_B * HD_B).astype(BF16)

    gf1 = row2(norm_ffn_g[1])
    h_p, c_p = _out_proj(o1_p, h_p, swa_w_out, gf1)
    h_s, c_s = _out_proj(o1_s, h_s, swa_w_out, gf1)

    gp1 = row2(norm_ple_g[1])
    h_p, n_p, *state_p1 = _conv_ffn(c_p, h_p, 1, ffn_wi, ffn_conv_w, conv_b, ffn_wo, gp1, None, tiles_per_seq)
    h_s, n_s, *state_s1 = _conv_ffn(c_s, h_s, 1, ffn_wi, ffn_conv_w, conv_b, ffn_wo, gp1, past, 1)

    (y_p,) = _ple(h_p, n_p, pp, 1, ple_w_proj, ple_w_gate, [])
    (y_s,) = _ple(h_s, n_s, ps, 1, ple_w_proj, ple_w_gate, [])

    to_bt = lambda a, *tail: a.reshape((T, NB) + tail).swapaxes(0, 1)
    y_prompt = y_p.reshape(B, S, D)
    y_sample = to_bt(y_s, D)
    fk_p = k_p.reshape(1, B, S, KVH_A, HD_A)
    fv_p = v_p.reshape(1, B, S, KVH_A, HD_A)
    flf_p = lf_p.reshape(1, B, S, H_A)
    last_win = lambda a: undup(a.reshape(B, S, -1)[:, S - WIN:].reshape(B * WIN, -1)).reshape(B, WIN, KVH_B, HD_B)
    win_k_p = last_win(kd_p)
    win_v_p = last_win(vd_p)
    conv_p = jnp.stack([jnp.concatenate([t[tiles_per_seq - 1::tiles_per_seq, HALO - (CONV_W - 1):] for t in st],
                                        axis=-1) for st in (state_p0, state_p1)])
    conv_s = jnp.stack([jnp.concatenate(st, axis=-1).swapaxes(0, 1) for st in (state_s0, state_s1)])
    fk_s = to_bt(k_s, KVH_A, HD_A)[None]
    fv_s = to_bt(v_s, KVH_A, HD_A)[None]
    flf_s = to_bt(lf_s, H_A)[None]
    win_k_s = jnp.concatenate([cache_win_k, ksh_s.reshape(NB, T, KVH_B, HD_B)], axis=1)[:, -WIN:]
    win_v_s = jnp.concatenate([cache_win_v, vsh_s.reshape(NB, T, KVH_B, HD_B)], axis=1)[:, -WIN:]
    return (y_prompt, y_sample, fk_p, fv_p, flf_p, win_k_p, win_v_p, conv_p,
            fk_s, fv_s, flf_s, win_k_s, win_v_s, conv_s)
```

```python
import functools

import jax
import jax.numpy as jnp
from jax import lax
from jax.experimental import pallas as pl
from jax.experimental.pallas import tpu as pltpu

F32 = jnp.float32
BF16 = jnp.bfloat16

EPS = 1e-6
PAGE_SIZE = 128
WINDOW = 128
CONV_W = 3
NEG = -1e30
MASK_DIST = 1e9
LOG2E = 1.4426950408889634
LANES = 128
MXU_N = 256
V7X_VMEM_LIMIT = 56 * 1024 * 1024
ROW_TILE = 512
FFN_ROWS = 512
FFN_TILE = 512
HALO = 16
FOX_TQ = 512
PAGES_PER_CHUNK = 16
DECODE_SLOTS = 4
SWA_SAMPLE_ROWS = 8
SCAN_BLOCK = 256
SCAN_ROWS = 128


def _cparams(*sem):
    return pltpu.CompilerParams(dimension_semantics=sem, vmem_limit_bytes=V7X_VMEM_LIMIT)


def _rms(x):
    return x * lax.rsqrt(jnp.mean(x * x, axis=-1, keepdims=True) + EPS)


def _dot(a, b):
    return jnp.dot(a, b, preferred_element_type=F32)


def _dot_nt(a, b):
    return lax.dot_general(a, b, (((1,), (1,)), ((), ())), preferred_element_type=F32)


def _split3(x):
    hi = x.astype(BF16)
    r = x - hi.astype(F32)
    mid = r.astype(BF16)
    lo = (r - mid.astype(F32)).astype(BF16)
    return hi, mid, lo


def _const_spec(shape):
    nd = len(shape)
    return pl.BlockSpec(shape, lambda *_: (0,) * nd)


def _weight_spec(w, layer=0):
    return pl.BlockSpec((None,) + w.shape[1:], lambda *_: (layer, 0, 0), pipeline_mode=pl.Buffered(1))


def _fox_proj_kernel(x_ref, g_ref, w_ref, bf_ref, qg_ref, kg_ref,
                     q_ref, k_ref, v_ref, kb_ref, vb_ref, lf_ref, *, nq, nkv):
    a = (_rms(x_ref[...]) * g_ref[...]).astype(BF16)
    tm = a.shape[0]
    heads = nkv // LANES
    qg = qg_ref[...]
    kg = kg_ref[...]
    chunk = lambda lo: _dot_nt(a, w_ref[lo:lo + MXU_N, :].astype(BF16))
    for c in range(nq // MXU_N):
        z = chunk(c * MXU_N)
        for s in range(MXU_N // LANES):
            lo = c * MXU_N + s * LANES
            q_ref[:, lo:lo + LANES] = (_rms(z[:, s * LANES:(s + 1) * LANES]) * qg).astype(BF16)
    for c in range(nkv // MXU_N):
        z = chunk(nq + c * MXU_N)
        for s in range(MXU_N // LANES):
            lo = c * MXU_N + s * LANES
            kn = _rms(z[:, s * LANES:(s + 1) * LANES]) * kg
            k_ref[pl.ds(lo // LANES, tm, stride=heads), :] = kn
            kb_ref[:, lo:lo + LANES] = kn.astype(BF16)
    for c in range(nkv // MXU_N):
        z = chunk(nq + nkv + c * MXU_N)
        for s in range(MXU_N // LANES):
            lo = c * MXU_N + s * LANES
            v_ref[pl.ds(lo // LANES, tm, stride=heads), :] = z[:, s * LANES:(s + 1) * LANES]
        vb_ref[:, c * MXU_N:(c + 1) * MXU_N] = z.astype(BF16)
    zf = _dot_nt(a, w_ref[nq + 2 * nkv:, :].astype(BF16)) + bf_ref[...]
    lf_ref[...] = -(jnp.maximum(-zf, 0.0) + jnp.log1p(jnp.exp(-jnp.abs(zf))))


def _fox_proj(x, g, w_in, bf, qg, kg, nq, nkv):
    m, d = x.shape
    nh = w_in.shape[1] - nq - 2 * nkv
    heads = nkv // LANES
    tm = min(ROW_TILE, m)
    row = lambda n: pl.BlockSpec((tm, n), lambda i: (i, 0))
    kv_f32 = pl.BlockSpec((tm * heads, LANES), lambda i: (i, 0))
    return pl.pallas_call(
        functools.partial(_fox_proj_kernel, nq=nq, nkv=nkv),
        grid=(m // tm,),
        in_specs=[row(d), _const_spec(g.shape), _weight_spec(w_in), _const_spec(bf.shape), _const_spec(qg.shape),
                  _const_spec(kg.shape)],
        out_specs=[row(nq), kv_f32, kv_f32, row(nkv), row(nkv), row(nh)],
        out_shape=[jax.ShapeDtypeStruct((m, nq), BF16), jax.ShapeDtypeStruct((m * heads, LANES), F32),
                   jax.ShapeDtypeStruct((m * heads, LANES), F32), jax.ShapeDtypeStruct((m, nkv), BF16),
                   jax.ShapeDtypeStruct((m, nkv), BF16), jax.ShapeDtypeStruct((m, nh), F32)],
        compiler_params=_cparams("arbitrary"),
        name="fox_proj",
    )(x, g, w_in, bf, qg, kg)


def _scan_kernel(x_ref, tri_ref, o_ref, *, suffix):
    tr, n = x_ref.shape
    nblk = n // SCAN_BLOCK
    tri = tri_ref[...]
    carry = jnp.zeros((tr, 1), F32)
    order = range(nblk - 1, -1, -1) if suffix else range(nblk)
    for blk in order:
        cols = slice(blk * SCAN_BLOCK, (blk + 1) * SCAN_BLOCK)
        x = x_ref[:, cols]
        r = _dot(jnp.concatenate(_split3(x), axis=0), tri)
        y = r[0:tr] + r[tr:2 * tr] + r[2 * tr:3 * tr] + carry
        if suffix:
            o_ref[:, cols] = y
            carry = y[:, 0:1] + x[:, 0:1]
        else:
            o_ref[:, cols] = -y
            carry = y[:, SCAN_BLOCK - 1:SCAN_BLOCK]


def _scan_lanes(x, suffix):
    rows, n = x.shape
    tr = min(SCAN_ROWS, rows)
    i = lax.broadcasted_iota(jnp.int32, (SCAN_BLOCK, SCAN_BLOCK), 0)
    j = lax.broadcasted_iota(jnp.int32, (SCAN_BLOCK, SCAN_BLOCK), 1)
    tri = jnp.where(i > j if suffix else i <= j, 1.0, 0.0).astype(BF16)
    blk = pl.BlockSpec((tr, n), lambda i: (i, 0))
    return pl.pallas_call(
        functools.partial(_scan_kernel, suffix=suffix),
        grid=(rows // tr,),
        in_specs=[blk, _const_spec(tri.shape)],
        out_specs=blk,
        out_shape=jax.ShapeDtypeStruct((rows, n), F32),
        compiler_params=_cparams("arbitrary"),
        name="logf_suffix" if suffix else "logf_prefix",
    )(x, tri)


def _fox_prompt_kernel(q_ref, k_ref, v_ref, b_ref, o_ref, qs_ref, m_ref, l_ref, acc_ref, *, group, hd):
    i = pl.program_id(2)
    t = q_ref.shape[0]
    rows = group * t
    for g in range(group):
        qs_ref[g * t:(g + 1) * t, :] = q_ref[:, g * hd:(g + 1) * hd]
    m_ref[...] = jnp.full(m_ref.shape, NEG, F32)
    l_ref[...] = jnp.zeros(l_ref.shape, F32)
    acc_ref[...] = jnp.zeros(acc_ref.shape, F32)

    def tile(kt, masked):
        k0 = pl.multiple_of(kt * t, t)
        s = _dot_nt(qs_ref[...], k_ref[pl.ds(k0, t), :]).reshape(group, t, t)
        s = s + (b_ref[0, 0, :, pl.ds(k0, t)] * LOG2E)[:, None, :]
        if masked:
            r = lax.broadcasted_iota(jnp.int32, (group, t, t), 1)
            c = lax.broadcasted_iota(jnp.int32, (group, t, t), 2)
            s = jnp.where(c <= r, s, NEG)
        s = s.reshape(rows, t)
        m_old = m_ref[...]
        m_new = jnp.maximum(m_old, jnp.max(s, axis=-1, keepdims=True))
        p = jnp.exp2(s - jnp.concatenate([m_new] * (t // LANES), axis=1))
        alpha = jnp.exp2(m_old - m_new)
        l_ref[...] = alpha * l_ref[...] + jnp.sum(p, axis=-1, keepdims=True)
        acc_ref[...] = alpha * acc_ref[...] + _dot(p.astype(BF16), v_ref[pl.ds(k0, t), :])
        m_ref[...] = m_new

    def body(kt, carry):
        tile(kt, False)
        return carry

    lax.fori_loop(0, i, body, 0)
    tile(i, True)
    o = acc_ref[...] / l_ref[...]
    for g in range(group):
        o_ref[:, g * hd:(g + 1) * hd] = o[g * t:(g + 1) * t].astype(o_ref.dtype)


def _fox_prompt_attn(q, kb, vb, bias, batch, seq, nkv, hd):
    assert hd == LANES
    group = q.shape[1] // (nkv * hd)
    t = FOX_TQ
    nq = seq // t
    return pl.pallas_call(
        functools.partial(_fox_prompt_kernel, group=group, hd=hd),
        grid=(batch, nkv, nq),
        in_specs=[pl.BlockSpec((t, group * hd), lambda b, h, i: (b * nq + i, h)),
                  pl.BlockSpec((seq, hd), lambda b, h, i: (b, h)),
                  pl.BlockSpec((seq, hd), lambda b, h, i: (b, h)),
                  pl.BlockSpec((1, 1, group, seq), lambda b, h, i: (b, h, 0, 0))],
        out_specs=pl.BlockSpec((t, group * hd), lambda b, h, i: (b * nq + i, h)),
        out_shape=jax.ShapeDtypeStruct(q.shape, BF16),
        scratch_shapes=[pltpu.VMEM((group * t, hd), BF16), pltpu.VMEM((group * t, LANES), F32),
                        pltpu.VMEM((group * t, LANES), F32), pltpu.VMEM((group * t, hd), F32)],
        compiler_params=_cparams("arbitrary", "arbitrary", "arbitrary"),
        name="fox_prompt_attn",
    )(q, kb, vb, bias)


def _fox_sample_kernel(pt_ref, q_ref, kn_ref, vn_ref, bn_ref, e_ref, tri_ref, kpool, vpool, lpool,
                       o_ref, kbuf, vbuf, lbuf, sem, *, nkv, hd, n_chunks, rows_q, group):
    b = pl.program_id(0)
    nb = pl.num_programs(0)
    page_rows = PAGE_SIZE * nkv
    lc = PAGES_PER_CHUNK * PAGE_SIZE
    nh = lbuf.shape[2]
    pages_per_block = SCAN_BLOCK // PAGE_SIZE
    nblk = lc // SCAN_BLOCK

    def start_chunk(bb, step, slot):
        c = n_chunks - 1 - step
        for p in range(PAGES_PER_CHUNK):
            page = pt_ref[bb, c * PAGES_PER_CHUNK + p]
            src = pl.ds(pl.multiple_of(page * page_rows, page_rows), page_rows)
            dst = pl.ds(p * page_rows, page_rows)
            pltpu.make_async_copy(kpool.at[src, :], kbuf.at[slot, dst, :], sem.at[0, slot]).start()
            pltpu.make_async_copy(vpool.at[src, :], vbuf.at[slot, dst, :], sem.at[1, slot]).start()
            pltpu.make_async_copy(lpool.at[page], lbuf.at[slot, p], sem.at[2, slot]).start()

    def wait_chunk(slot):
        for buf, s in ((kbuf, 0), (vbuf, 1), (lbuf, 2)):
            pltpu.make_async_copy(buf.at[slot], buf.at[slot], sem.at[s, slot]).wait()

    ahead = DECODE_SLOTS - 1

    @pl.when(b == 0)
    def _():
        for g in range(ahead):
            start_chunk(g // n_chunks, g % n_chunks, g % DECODE_SLOTS)

    e = e_ref[...]
    tri = tri_ref[...]
    q = [q_ref[0, h] for h in range(nkv)]

    def chunk(step, carry):
        g = b * n_chunks + step
        slot = lax.rem(g, DECODE_SLOTS)
        nxt = g + ahead

        @pl.when(nxt < nb * n_chunks)
        def _():
            start_chunk(lax.div(nxt, n_chunks), lax.rem(nxt, n_chunks), lax.rem(nxt, DECODE_SLOTS))

        wait_chunk(slot)

        xs = [jnp.concatenate([lbuf[slot, k * pages_per_block + p] for p in range(pages_per_block)], axis=1)
              for k in range(nblk)]
        r = _dot(jnp.concatenate([part for x in xs for part in _split3(x)], axis=0), tri)
        sfx_carry = carry[nkv]
        ys = [None] * nblk
        for k in range(nblk - 1, -1, -1):
            local = r[(3 * k) * nh:(3 * k + 1) * nh] + r[(3 * k + 1) * nh:(3 * k + 2) * nh] \
                + r[(3 * k + 2) * nh:(3 * k + 3) * nh]
            ys[k] = local + sfx_carry
            sfx_carry = sfx_carry + (local[:, 0:1] + xs[k][:, 0:1])
        bias = _dot(e, jnp.concatenate(_split3(jnp.concatenate(ys, axis=1) * LOG2E), axis=0))
        new = []
        for h in range(nkv):
            m_old, l_old, acc_old = carry[h]
            k = kbuf[slot, pl.ds(h, lc, stride=nkv), :].astype(BF16)
            v = vbuf[slot, pl.ds(h, lc, stride=nkv), :].astype(BF16)
            s = _dot_nt(q[h], k) + bias[h * rows_q:(h + 1) * rows_q]
            m_new = jnp.maximum(m_old, jnp.max(s, axis=-1, keepdims=True))
            p = jnp.exp2(s - m_new)
            alpha = jnp.exp2(m_old - m_new)
            new.append((m_new, alpha * l_old + jnp.sum(p, axis=-1, keepdims=True),
                        alpha * acc_old + _dot(p.astype(BF16), v)))
        return tuple(new) + (sfx_carry,)

    init = tuple((jnp.full((rows_q, 1), NEG, F32), jnp.zeros((rows_q, 1), F32),
                  jnp.zeros((rows_q, hd), F32)) for _ in range(nkv)) + (jnp.zeros((nh, 1), F32),)
    state = lax.fori_loop(0, n_chunks, chunk, init)

    tn = kn_ref.shape[1]
    bias_n = _dot_nt(e, bn_ref[0])
    r = lax.broadcasted_iota(jnp.int32, (rows_q, tn), 0)
    c = lax.broadcasted_iota(jnp.int32, (rows_q, tn), 1)
    ok = c * group <= r
    for h in range(nkv):
        m_old, l_old, acc_old = state[h]
        k = kn_ref[0, :, h * hd:(h + 1) * hd].astype(BF16)
        v = vn_ref[0, :, h * hd:(h + 1) * hd].astype(BF16)
        s = jnp.where(ok, _dot_nt(q[h], k) + bias_n[h * rows_q:(h + 1) * rows_q], NEG)
        m_new = jnp.maximum(m_old, jnp.max(s, axis=-1, keepdims=True))
        p = jnp.exp2(s - m_new)
        alpha = jnp.exp2(m_old - m_new)
        l_new = alpha * l_old + jnp.sum(p, axis=-1, keepdims=True)
        acc = alpha * acc_old + _dot(p.astype(BF16), v)
        o_ref[0, h] = (acc / l_new).astype(o_ref.dtype)


def _fox_sample_attn(page_table, q, k_new, v_new, bias_new3, emat, kpool, vpool, lpool, nkv, hd, group):
    nb, _, rows_q, _ = q.shape
    n_pages = page_table.shape[1]
    nh = lpool.shape[1]
    n_chunks = n_pages // PAGES_PER_CHUNK
    buf_rows = PAGES_PER_CHUNK * PAGE_SIZE * nkv
    tn = k_new.shape[1]
    i = lax.broadcasted_iota(jnp.int32, (SCAN_BLOCK, SCAN_BLOCK), 0)
    j = lax.broadcasted_iota(jnp.int32, (SCAN_BLOCK, SCAN_BLOCK), 1)
    tri = jnp.where(i > j, 1.0, 0.0).astype(BF16)
    const = lambda a: pl.BlockSpec(a.shape, lambda b, pt: (0,) * a.ndim)
    grid_spec = pltpu.PrefetchScalarGridSpec(
        num_scalar_prefetch=1,
        grid=(nb,),
        in_specs=[pl.BlockSpec((1, nkv, rows_q, hd), lambda b, pt: (b, 0, 0, 0)),
                  pl.BlockSpec((1, tn, nkv * hd), lambda b, pt: (b, 0, 0)),
                  pl.BlockSpec((1, tn, nkv * hd), lambda b, pt: (b, 0, 0)),
                  pl.BlockSpec((1,) + bias_new3.shape[1:], lambda b, pt: (b, 0, 0)),
                  const(emat), const(tri),
                  pl.BlockSpec(memory_space=pl.ANY),
                  pl.BlockSpec(memory_space=pl.ANY),
                  pl.BlockSpec(memory_space=pl.ANY)],
        out_specs=pl.BlockSpec((1, nkv, rows_q, hd), lambda b, pt: (b, 0, 0, 0)),
        scratch_shapes=[pltpu.VMEM((DECODE_SLOTS, buf_rows, hd), F32), pltpu.VMEM((DECODE_SLOTS, buf_rows, hd), F32),
                        pltpu.VMEM((DECODE_SLOTS, PAGES_PER_CHUNK, nh, PAGE_SIZE), F32),
                        pltpu.SemaphoreType.DMA((3, DECODE_SLOTS))],
    )
    return pl.pallas_call(
        functools.partial(_fox_sample_kernel, nkv=nkv, hd=hd, n_chunks=n_chunks, rows_q=rows_q, group=group),
        grid_spec=grid_spec,
        out_shape=jax.ShapeDtypeStruct(q.shape, BF16),
        compiler_params=_cparams("arbitrary"),
        name="fox_sample_attn",
    )(page_table, q, k_new, v_new, bias_new3, emat, tri, kpool, vpool, lpool)


def _out_proj_kernel(o_ref, h_ref, w_ref, g_ref, h1_ref, c_ref):
    o = o_ref[...]
    tn = 512
    for c in range(h_ref.shape[1] // tn):
        cs = slice(c * tn, (c + 1) * tn)
        h1_ref[:, cs] = h_ref[:, cs] + _dot(o, w_ref[:, cs].astype(BF16))
    c_ref[...] = (_rms(h1_ref[...]) * g_ref[...]).astype(BF16)


def _out_proj(o, h, w, g):
    m, d = h.shape
    tm = min(ROW_TILE, m)
    row = lambda n: pl.BlockSpec((tm, n), lambda i: (i, 0))
    return pl.pallas_call(
        _out_proj_kernel,
        grid=(m // tm,),
        in_specs=[row(o.shape[1]), row(d), _weight_spec(w), _const_spec(g.shape)],
        out_specs=[row(d), row(d)],
        out_shape=[jax.ShapeDtypeStruct((m, d), F32), jax.ShapeDtypeStruct((m, d), BF16)],
        compiler_params=_cparams("arbitrary"),
        name="attn_out_proj",
    )(o, h, w, g)


def _ffn_begin(j, acc_ref):
    @pl.when(j == 0)
    def _():
        acc_ref[...] = jnp.zeros(acc_ref.shape, F32)


def _ffn_epilogue(j, h_ref, gn_ref, acc_ref, h2_ref, n_ref):
    @pl.when(j == pl.num_programs(1) - 1)
    def _():
        h2 = h_ref[...] + acc_ref[...]
        h2_ref[...] = h2
        n_ref[...] = (_rms(h2) * gn_ref[...]).astype(BF16)


def _ffn_prompt_kernel(c_ref, wg_ref, wv_ref, cwg_ref, cwv_ref, cbg_ref, cbv_ref, wo_ref, h_ref, gn_ref,
                       h2_ref, n_ref, tg_ref, tv_ref, acc_ref, ug_ref, uv_ref, pg_ref, pv_ref, *, tiles_per_seq):
    i, j = pl.program_id(0), pl.program_id(1)
    tm = c_ref.shape[0]
    first = lax.rem(i, tiles_per_seq) == 0
    _ffn_begin(j, acc_ref)

    @pl.when((i == 0) & (j == 0))
    def _():
        pg_ref[...] = jnp.zeros(pg_ref.shape, F32)
        pv_ref[...] = jnp.zeros(pv_ref.shape, F32)

    x = c_ref[...]
    for u_ref, p_ref, w_ref, t_ref in ((ug_ref, pg_ref, wg_ref, tg_ref), (uv_ref, pv_ref, wv_ref, tv_ref)):
        u_ref[0:HALO, :] = jnp.where(first, 0.0, p_ref[j])
        u_ref[HALO:, :] = _dot(x, w_ref[...])
        tail = u_ref[tm:tm + HALO, :]
        p_ref[j] = tail
        t_ref[...] = tail

    def conv(u_ref, cw_ref, cb_ref):
        y = cb_ref[...]
        for k in range(CONV_W):
            y = y + cw_ref[k:k + 1, :] * u_ref[pl.ds(HALO - (CONV_W - 1) + k, tm), :]
        return y

    act = (jax.nn.gelu(conv(ug_ref, cwg_ref, cbg_ref), approximate=True) * conv(uv_ref, cwv_ref, cbv_ref)).astype(BF16)
    acc_ref[...] += _dot(act, wo_ref[...].astype(BF16))
    _ffn_epilogue(j, h_ref, gn_ref, acc_ref, h2_ref, n_ref)


def _ffn_sample_kernel(c_ref, pg_ref, pv_ref, wg_ref, wv_ref, cwg_ref, cwv_ref, cbg_ref, cbv_ref, wo_ref,
                       h_ref, gn_ref, h2_ref, n_ref, tg_ref, tv_ref, acc_ref):
    j = pl.program_id(1)
    nb = pg_ref.shape[1]
    nt = c_ref.shape[0] // nb
    _ffn_begin(j, acc_ref)
    x = c_ref[...]

    def conv(u, p_ref, cw_ref, cb_ref, t_ref):
        slabs = [p_ref[k] for k in range(CONV_W - 1)] + [u[t * nb:(t + 1) * nb] for t in range(nt)]
        for k in range(CONV_W - 1):
            t_ref[k] = slabs[nt + k]
        y = cb_ref[...]
        for k in range(CONV_W):
            y = y + cw_ref[k:k + 1, :] * jnp.concatenate(slabs[k:k + nt], axis=0)
        return y

    g = conv(_dot(x, wg_ref[...]), pg_ref, cwg_ref, cbg_ref, tg_ref)
    v = conv(_dot(x, wv_ref[...]), pv_ref, cwv_ref, cbv_ref, tv_ref)
    act = (jax.nn.gelu(g, approximate=True) * v).astype(BF16)
    acc_ref[...] += _dot(act, wo_ref[...].astype(BF16))
    _ffn_epilogue(j, h_ref, gn_ref, acc_ref, h2_ref, n_ref)


def _conv_ffn(c, h, layer, w_in, conv_w, conv_b, w_out, g_next, past, tiles_per_seq):
    m, d = h.shape
    dff = w_out.shape[1]
    tf = FFN_TILE
    nf = dff // tf
    tm = min(FFN_ROWS, m)
    wspecs = [pl.BlockSpec((None, d, tf), lambda i, j: (layer, 0, j)),
              pl.BlockSpec((None, d, tf), lambda i, j: (layer, 0, nf + j)),
              pl.BlockSpec((None, CONV_W, tf), lambda i, j: (layer, 0, j)),
              pl.BlockSpec((None, CONV_W, tf), lambda i, j: (layer, 0, nf + j)),
              pl.BlockSpec((None, 1, tf), lambda i, j: (layer, 0, j)),
              pl.BlockSpec((None, 1, tf), lambda i, j: (layer, 0, nf + j)),
              pl.BlockSpec((None, tf, d), lambda i, j: (layer, j, 0))]
    wargs = [w_in, w_in, conv_w, conv_w, conv_b, conv_b, w_out]
    row = pl.BlockSpec((tm, d), lambda i, j: (i, 0))
    tail_specs = [row, pl.BlockSpec((1, d), lambda i, j: (0, 0))]
    if past is None:
        head_specs = [row]
        head_args = [c]
        kern = functools.partial(_ffn_prompt_kernel, tiles_per_seq=tiles_per_seq)
        scratch = [pltpu.VMEM((tm + HALO, tf), F32)] * 2 + [pltpu.VMEM((nf, HALO, tf), F32)] * 2
        state = jax.ShapeDtypeStruct((m // tm, HALO, dff), F32)
        state_spec = pl.BlockSpec((None, HALO, tf), lambda i, j: (i, 0, j))
    else:
        scratch = []
        nb = past.shape[2]
        head_specs = [row, pl.BlockSpec((None, CONV_W - 1, nb, tf), lambda i, j: (layer, 0, 0, j)),
                      pl.BlockSpec((None, CONV_W - 1, nb, tf), lambda i, j: (layer, 0, 0, nf + j))]
        head_args = [c, past, past]
        kern = _ffn_sample_kernel
        state = jax.ShapeDtypeStruct((CONV_W - 1, nb, dff), F32)
        state_spec = pl.BlockSpec((CONV_W - 1, nb, tf), lambda i, j: (0, 0, j))
    return pl.pallas_call(
        kern,
        grid=(m // tm, nf),
        in_specs=head_specs + wspecs + tail_specs,
        out_specs=[row, row, state_spec, state_spec],
        out_shape=[jax.ShapeDtypeStruct((m, d), F32), jax.ShapeDtypeStruct((m, d), BF16), state, state],
        scratch_shapes=[pltpu.VMEM((tm, d), F32)] + scratch,
        compiler_params=_cparams("arbitrary", "arbitrary"),
        name="conv_ffn_prompt" if past is None else "conv_ffn_sample",
    )(*head_args, *wargs, h, g_next)


def _ple_kernel(h_ref, n_ref, p_ref, wp_ref, wg_ref, *rest, n_norms):
    g_refs = rest[:n_norms]
    h3_ref = rest[n_norms]
    a_refs = rest[n_norms + 1:]
    n = n_ref[...]
    p = p_ref[...].astype(BF16)
    d = h_ref.shape[1]
    tn = 512
    for c in range(d // tn):
        cs = slice(c * tn, (c + 1) * tn)
        gate = jax.nn.sigmoid(_dot(n, wg_ref[:, cs].astype(BF16)))
        h3_ref[:, cs] = h_ref[:, cs] + _dot(p, wp_ref[:, cs].astype(BF16)) * gate
    if n_norms:
        xhat = _rms(h3_ref[...])
        for g_ref, a_ref in zip(g_refs, a_refs):
            a_ref[...] = (xhat * g_ref[...]).astype(BF16)


def _ple(h, n, p, layer, wp, wg, gains):
    m, d = h.shape
    tm = min(ROW_TILE, m)
    row = lambda w: pl.BlockSpec((tm, w), lambda i: (i, 0))
    k = len(gains)
    outs = pl.pallas_call(
        functools.partial(_ple_kernel, n_norms=k),
        grid=(m // tm,),
        in_specs=[row(d), row(d), pl.BlockSpec((None, tm, p.shape[2]), lambda i: (layer, i, 0)),
                  _weight_spec(wp, layer), _weight_spec(wg, layer)]
                 + [_const_spec(g.shape) for g in gains],
        out_specs=[row(d)] * (k + 1),
        out_shape=[jax.ShapeDtypeStruct((m, d), F32)] + [jax.ShapeDtypeStruct((m, d), BF16)] * k,
        compiler_params=_cparams("arbitrary"),
        name="ple",
    )(h, n, p, wp, wg, *gains)
    return outs


def _swa_proj_kernel(a_ref, akv_ref, wq_ref, wkv_ref, qg_ref, kg_ref, q_ref, k_ref, v_ref, *, nkv, hd):
    a = a_ref[...]
    qg = qg_ref[...]
    lane = lax.broadcasted_iota(jnp.int32, (a.shape[0], LANES), 1)
    low = lane < hd
    for c in range(wq_ref.shape[1] // MXU_N):
        z = _dot(a, wq_ref[:, c * MXU_N:(c + 1) * MXU_N].astype(BF16))
        for s in range(MXU_N // LANES):
            zz = z[:, s * LANES:(s + 1) * LANES]
            sq = zz * zz
            s_lo = jnp.sum(jnp.where(low, sq, 0.0), axis=-1, keepdims=True)
            s_hi = jnp.sum(jnp.where(low, 0.0, sq), axis=-1, keepdims=True)
            inv = jnp.where(low, lax.rsqrt(s_lo / hd + EPS), lax.rsqrt(s_hi / hd + EPS))
            lo = c * MXU_N + s * LANES
            q_ref[:, lo:lo + LANES] = (zz * inv * qg).astype(BF16)
    akv = akv_ref[...]
    kg = kg_ref[...]
    for c in range(nkv // MXU_N):
        z = _dot(akv, wkv_ref[:, c * MXU_N:(c + 1) * MXU_N])
        for s in range(MXU_N // LANES):
            lo = c * MXU_N + s * LANES
            k_ref[:, lo:lo + LANES] = _rms(z[:, s * LANES:(s + 1) * LANES]) * kg
    for c in range(nkv // MXU_N):
        v_ref[:, c * MXU_N:(c + 1) * MXU_N] = _dot(akv, wkv_ref[:, nkv + c * MXU_N:nkv + (c + 1) * MXU_N])


def _swa_proj(a, akv, wq, wkv_dup, qg, kg, hd):
    m, d = a.shape
    nkv = wkv_dup.shape[2] // 2
    nq = wq.shape[2]
    tm = min(ROW_TILE, m)
    row = lambda n: pl.BlockSpec((tm, n), lambda i: (i, 0))
    return pl.pallas_call(
        functools.partial(_swa_proj_kernel, nkv=nkv, hd=hd),
        grid=(m // tm,),
        in_specs=[row(d), row(d), _weight_spec(wq), _weight_spec(wkv_dup),
                  _const_spec(qg.shape), _const_spec(kg.shape)],
        out_specs=[row(nq), row(nkv), row(nkv)],
        out_shape=[jax.ShapeDtypeStruct((m, nq), BF16), jax.ShapeDtypeStruct((m, nkv), F32),
                   jax.ShapeDtypeStruct((m, nkv), F32)],
        compiler_params=_cparams("arbitrary"),
        name="swa_proj",
    )(a, akv, wq, wkv_dup, qg, kg)


def _swa_prompt_kernel(slope_ref, sink_ref, q_ref, kp_ref, kc_ref, vp_ref, vc_ref, o_ref, *, nkv, hd):
    i = pl.program_id(1)
    w = q_ref.shape[0]
    pairs = q_ref.shape[1] // (nkv * LANES)
    lane = lax.broadcasted_iota(jnp.int32, (w, LANES), 1)
    low = lane < hd
    r = lax.broadcasted_iota(jnp.int32, (w, 2 * w), 0)
    c = lax.broadcasted_iota(jnp.int32, (w, 2 * w), 1)
    dist = r + w - c
    valid = (dist >= 0) & (dist < w) & ((c >= w) | (i > 0))
    base = jnp.where(valid, dist.astype(F32), MASK_DIST)
    top = lax.broadcasted_iota(jnp.int32, (2 * w, 1), 0) < w
    for h in range(nkv):
        k = jnp.concatenate([kp_ref[:, h * LANES:(h + 1) * LANES], kc_ref[:, h * LANES:(h + 1) * LANES]],
                            axis=0).astype(BF16)
        v = jnp.concatenate([vp_ref[:, h * LANES:(h + 1) * LANES], vc_ref[:, h * LANES:(h + 1) * LANES]],
                            axis=0).astype(BF16)
        for pr in range(pairs):
            col = (h * pairs + pr) * LANES
            h0 = 2 * (h * pairs + pr)
            qp = q_ref[:, col:col + LANES]
            rows = jnp.concatenate([jnp.where(low, qp, jnp.zeros_like(qp)),
                                    jnp.where(low, jnp.zeros_like(qp), qp)], axis=0)
            s = _dot_nt(rows, k) - jnp.concatenate([slope_ref[h0] * base, slope_ref[h0 + 1] * base], axis=0)
            sink = jnp.where(top, sink_ref[h0], sink_ref[h0 + 1])
            m = jnp.maximum(jnp.max(s, axis=-1, keepdims=True), sink)
            p = jnp.exp2(s - m)
            den = jnp.sum(p, axis=-1, keepdims=True) + jnp.exp2(sink - m)
            o = _dot(p.astype(BF16), v) / den
            o_ref[:, col:col + LANES] = jnp.where(low, o[:w], o[w:]).astype(o_ref.dtype)


def _swa_prompt_attn(q, kdup, vdup, slopes, sinks, batch, seq, nkv, hd):
    nblk = seq // WINDOW
    dq = q.shape[1]
    dk = kdup.shape[1]
    cur = lambda b, i: (b * nblk + i, 0)
    prev = lambda b, i: (b * nblk + jnp.maximum(i - 1, 0), 0)
    smem = pl.BlockSpec(memory_space=pltpu.SMEM)
    return pl.pallas_call(
        functools.partial(_swa_prompt_kernel, nkv=nkv, hd=hd),
        grid=(batch, nblk),
        in_specs=[smem, smem, pl.BlockSpec((WINDOW, dq), cur), pl.BlockSpec((WINDOW, dk), prev),
                  pl.BlockSpec((WINDOW, dk), cur), pl.BlockSpec((WINDOW, dk), prev),
                  pl.BlockSpec((WINDOW, dk), cur)],
        out_specs=pl.BlockSpec((WINDOW, dq), cur),
        out_shape=jax.ShapeDtypeStruct(q.shape, BF16),
        compiler_params=_cparams("arbitrary", "arbitrary"),
        name="swa_prompt_attn",
    )(slopes, sinks, q, kdup, kdup, vdup, vdup)


def _swa_sample_kernel(q_ref, kc_ref, vc_ref, kn_ref, vn_ref, slope_ref, sink_ref, o_ref, *, heads):
    rows = q_ref.shape[1]
    w = kc_ref.shape[2]
    tn = kn_ref.shape[1]
    slope = slope_ref[...]
    sink = sink_ref[...]
    t_p = lax.broadcasted_iota(jnp.int32, (rows, w), 0) // heads
    j_p = lax.broadcasted_iota(jnp.int32, (rows, w), 1)
    dist_p = t_p + w - j_p
    bias_p = jnp.where(dist_p < WINDOW, slope * dist_p.astype(F32), -NEG)
    t_n = lax.broadcasted_iota(jnp.int32, (rows, tn), 0) // heads
    j_n = lax.broadcasted_iota(jnp.int32, (rows, tn), 1)
    dist_n = t_n - j_n
    bias_n = jnp.where(dist_n >= 0, slope * dist_n.astype(F32), -NEG)
    for b in range(q_ref.shape[0]):
        q = q_ref[b]
        s_p = _dot(q, kc_ref[b].astype(BF16)) - bias_p
        s_n = _dot_nt(q, kn_ref[b].astype(BF16)) - bias_n
        m = jnp.maximum(jnp.maximum(jnp.max(s_p, axis=-1, keepdims=True), jnp.max(s_n, axis=-1, keepdims=True)),
                        sink)
        p_p = jnp.exp2(s_p - m)
        p_n = jnp.exp2(s_n - m)
        den = jnp.sum(p_p, axis=-1, keepdims=True) + jnp.sum(p_n, axis=-1, keepdims=True) + jnp.exp2(sink - m)
        o = _dot_nt(p_p.astype(BF16), vc_ref[b].astype(BF16)) + _dot(p_n.astype(BF16), vn_ref[b].astype(BF16))
        o_ref[b] = o / den


def _swa_sample_attn(q, kct, vct, kn, vn, slope_col, sink_col, heads):
    nb, rows, dk = q.shape
    g = SWA_SAMPLE_ROWS
    blk = lambda a: pl.BlockSpec((g,) + a.shape[1:], lambda b: (b, 0, 0))
    return pl.pallas_call(
        functools.partial(_swa_sample_kernel, heads=heads),
        grid=(nb // g,),
        in_specs=[blk(q), blk(kct), blk(vct), blk(kn), blk(vn), _const_spec(slope_col.shape),
                  _const_spec(sink_col.shape)],
        out_specs=blk(q),
        out_shape=jax.ShapeDtypeStruct((nb, rows, dk), F32),
        compiler_params=_cparams("arbitrary"),
        name="swa_sample_attn",
    )(q, kct, vct, kn, vn, slope_col, sink_col)


def kernel(x_prompt, x_sample, cache_fox_k, cache_fox_v, cache_fox_logf, cache_win_k, cache_win_v, state_conv, page_table, p_prompt, p_sample, norm_attn_g, norm_ffn_g, norm_ple_g, fox_w_in, fox_b_f, fox_q_norm_g, fox_k_norm_g, fox_w_out, kv_norm_g, swa_w_kv, swa_k_norm_g, swa_w_q, swa_q_norm_g, swa_sinks, swa_w_out, ffn_w_in, ffn_conv_w, ffn_conv_b, ffn_w_out, ple_w_proj, ple_w_gate):
    B, S, D = x_prompt.shape
    NB, T, _ = x_sample.shape
    HD_A = fox_q_norm_g.shape[-1]
    H_A = fox_b_f.shape[-1]
    KVH_A = cache_fox_k.shape[3]
    G_A = H_A // KVH_A
    HD_B = swa_q_norm_g.shape[-1]
    H_B = swa_sinks.shape[-1]
    KVH_B = cache_win_k.shape[2]
    G_B = H_B // KVH_B
    WIN = cache_win_k.shape[1]
    L = page_table.shape[1] * PAGE_SIZE
    DFF = ffn_w_out.shape[1]
    NQ_A, NK_A = H_A * HD_A, KVH_A * HD_A
    row2 = lambda g: g.reshape(1, -1)

    bf_a = fox_b_f
    qg_a = row2(fox_q_norm_g[0] * (HD_A ** -0.5 * LOG2E))
    kg_a = row2(fox_k_norm_g[0])
    dup = lambda w: jnp.repeat(w.reshape(D, KVH_B, 1, HD_B), LANES // HD_B, axis=2).reshape(D, KVH_B * LANES)
    wk_b, wv_b = jnp.split(swa_w_kv, 2, axis=-1)
    wkv_b = jnp.concatenate([dup(wk_b), dup(wv_b)], axis=1).astype(BF16)[None]
    qg_b = row2(jnp.tile(swa_q_norm_g[0] * (HD_B ** -0.5 * LOG2E), LANES // HD_B))
    kg_b = row2(jnp.tile(swa_k_norm_g, LANES // HD_B))
    ffn_wi = ffn_w_in.astype(BF16)
    ffn_wo = ffn_w_out
    slopes = jnp.exp2(-8.0 * jnp.arange(1, H_B + 1, dtype=F32) / H_B) * LOG2E
    sinks = swa_sinks[0] * LOG2E

    xp = x_prompt.reshape(B * S, D)
    xs = x_sample.transpose(1, 0, 2).reshape(T * NB, D)
    pp = p_prompt.reshape(2, B * S, -1)
    ps = p_sample.transpose(0, 2, 1, 3).reshape(2, T * NB, -1)
    past = state_conv.transpose(0, 2, 1, 3)

    g0 = row2(norm_attn_g[0])
    w_in_t = fox_w_in.transpose(0, 2, 1)
    q_p, k_p, v_p, kb_p, vb_p, lf_p = _fox_proj(xp, g0, w_in_t, bf_a, qg_a, kg_a, NQ_A, NK_A)
    q_s, k_s, v_s, _, _, lf_s = _fox_proj(xs, g0, w_in_t, bf_a, qg_a, kg_a, NQ_A, NK_A)

    bias_p = _scan_lanes(lf_p.reshape(B, S, H_A).transpose(0, 2, 1).reshape(B * H_A, S), suffix=False)
    o_p = _fox_prompt_attn(q_p, kb_p, vb_p, bias_p.reshape(B, KVH_A, G_A, S), B, S, KVH_A, HD_A)

    tn = 16
    lf_new = lf_s.reshape(T, NB, H_A).transpose(1, 2, 0).reshape(NB * H_A, T)
    bias_new = _scan_lanes(jnp.pad(lf_new, ((0, 0), (0, SCAN_BLOCK - T))), suffix=False)[:, :tn]
    bias_new3 = jnp.concatenate(_split3(bias_new.reshape(NB, H_A, tn).transpose(0, 2, 1) * LOG2E), axis=-1)
    rows_q = T * G_A
    head_of_row = (jnp.arange(KVH_A)[:, None] * G_A + jnp.arange(rows_q)[None, :] % G_A).reshape(-1)
    emat = jnp.tile(jax.nn.one_hot(head_of_row, H_A, dtype=BF16), (1, 3))
    qs_b = q_s.reshape(T, NB, KVH_A, G_A, HD_A).transpose(1, 2, 0, 3, 4).reshape(NB, KVH_A, rows_q, HD_A)
    pad_new = lambda a: jnp.pad(a.reshape(T, NB, NK_A).transpose(1, 0, 2), ((0, 0), (0, tn - T), (0, 0)))
    kpool = cache_fox_k.reshape(-1, HD_A)
    vpool = cache_fox_v.reshape(-1, HD_A)
    o_s = _fox_sample_attn(page_table, qs_b, pad_new(k_s), pad_new(v_s), bias_new3, emat, kpool, vpool,
                           cache_fox_logf.reshape(cache_fox_logf.shape[1:]).transpose(0, 2, 1), KVH_A, HD_A, G_A)
    o_s = o_s.reshape(NB, KVH_A, T, G_A, HD_A).transpose(2, 0, 1, 3, 4).reshape(T * NB, NQ_A)

    gf0 = row2(norm_ffn_g[0])
    h_p, c_p = _out_proj(o_p, xp, fox_w_out, gf0)
    h_s, c_s = _out_proj(o_s, xs, fox_w_out, gf0)

    gp0 = row2(norm_ple_g[0])
    tiles_per_seq = S // FFN_ROWS
    conv_b = ffn_conv_b[:, None, :]
    h_p, n_p, *state_p0 = _conv_ffn(c_p, h_p, 0, ffn_wi, ffn_conv_w, conv_b, ffn_wo, gp0, None, tiles_per_seq)
    h_s, n_s, *state_s0 = _conv_ffn(c_s, h_s, 0, ffn_wi, ffn_conv_w, conv_b, ffn_wo, gp0, past, 1)

    gq1, gkv = row2(norm_attn_g[1]), row2(kv_norm_g)
    h_p, a_p, akv_p = _ple(h_p, n_p, pp, 0, ple_w_proj, ple_w_gate, [gq1, gkv])
    h_s, a_s, akv_s = _ple(h_s, n_s, ps, 0, ple_w_proj, ple_w_gate, [gq1, gkv])

    q1_p, kd_p, vd_p = _swa_proj(a_p, akv_p, swa_w_q, wkv_b, qg_b, kg_b, HD_B)
    q1_s, kd_s, vd_s = _swa_proj(a_s, akv_s, swa_w_q, wkv_b, qg_b, kg_b, HD_B)
    undup = lambda a: a.reshape(a.shape[0], KVH_B, LANES // HD_B, HD_B)[:, :, 0]
    o1_p = _swa_prompt_attn(q1_p, kd_p, vd_p, slopes, sinks, B, S, KVH_B, HD_B)

    ksh_s = undup(kd_s).reshape(T, NB, KVH_B * HD_B).transpose(1, 0, 2)
    vsh_s = undup(vd_s).reshape(T, NB, KVH_B * HD_B).transpose(1, 0, 2)
    q1h = q1_s.reshape(T, NB, H_B, 1, HD_B).transpose(1, 0, 2, 3, 4)
    slab = (jnp.arange(H_B)[:, None] // G_B == jnp.arange(KVH_B)[None, :])[None, None, :, :, None]
    q1x = jnp.where(slab, q1h, jnp.zeros((), BF16)).reshape(NB, T * H_B, KVH_B * HD_B)
    padn = lambda a: jnp.pad(a, ((0, 0), (0, tn - T), (0, 0)))
    slope_col = jnp.tile(slopes, T).reshape(T * H_B, 1)
    sink_col = jnp.tile(sinks, T).reshape(T * H_B, 1)
    feature_major = lambda a: a.transpose(0, 2, 3, 1).reshape(NB, KVH_B * HD_B, WIN)
    o1x = _swa_sample_attn(q1x, feature_major(cache_win_k), feature_major(cache_win_v),
                           padn(ksh_s), padn(vsh_s), slope_col, sink_col, H_B)
    o1x = o1x.reshape(NB, T, H_B, KVH_B, HD_B)
    o1_s = jnp.sum(jnp.where(slab, o1x, 0.0), axis=3)
    o1_s = o1_s.transpose(1, 0, 2, 3).reshape(T * NB, H_B * HD_B).astype(BF16)

    gf1 = row2(norm_ffn_g[1])
    h_p, c_p = _out_proj(o1_p, h_p, swa_w_out, gf1)
    h_s, c_s = _out_proj(o1_s, h_s, swa_w_out, gf1)

    gp1 = row2(norm_ple_g[1])
    h_p, n_p, *state_p1 = _conv_ffn(c_p, h_p, 1, ffn_wi, ffn_conv_w, conv_b, ffn_wo, gp1, None, tiles_per_seq)
    h_s, n_s, *state_s1 = _conv_ffn(c_s, h_s, 1, ffn_wi, ffn_conv_w, conv_b, ffn_wo, gp1, past, 1)

    (y_p,) = _ple(h_p, n_p, pp, 1, ple_w_proj, ple_w_gate, [])
    (y_s,) = _ple(h_s, n_s, ps, 1, ple_w_proj, ple_w_gate, [])

    to_bt = lambda a, *tail: a.reshape((T, NB) + tail).swapaxes(0, 1)
    y_prompt = y_p.reshape(B, S, D)
    y_sample = to_bt(y_s, D)
    fk_p = k_p.reshape(1, B, S, KVH_A, HD_A)
    fv_p = v_p.reshape(1, B, S, KVH_A, HD_A)
    flf_p = lf_p.reshape(1, B, S, H_A)
    last_win = lambda a: undup(a.reshape(B, S, -1)[:, S - WIN:].reshape(B * WIN, -1)).reshape(B, WIN, KVH_B, HD_B)
    win_k_p = last_win(kd_p)
    win_v_p = last_win(vd_p)
    conv_p = jnp.stack([jnp.concatenate([t[tiles_per_seq - 1::tiles_per_seq, HALO - (CONV_W - 1):] for t in st],
                                        axis=-1) for st in (state_p0, state_p1)])
    conv_s = jnp.stack([jnp.concatenate(st, axis=-1).swapaxes(0, 1) for st in (state_s0, state_s1)])
    fk_s = to_bt(k_s, KVH_A, HD_A)[None]
    fv_s = to_bt(v_s, KVH_A, HD_A)[None]
    flf_s = to_bt(lf_s, H_A)[None]
    win_k_s = jnp.concatenate([cache_win_k, ksh_s.reshape(NB, T, KVH_B, HD_B)], axis=1)[:, -WIN:]
    win_v_s = jnp.concatenate([cache_win_v, vsh_s.reshape(NB, T, KVH_B, HD_B)], axis=1)[:, -WIN:]
    return (y_prompt, y_sample, fk_p, fv_p, flf_p, win_k_p, win_v_p, conv_p,
            fk_s, fv_s, flf_s, win_k_s, win_v_s, conv_s)
```

```python
import functools

import jax
import jax.numpy as jnp
from jax import lax
from jax.experimental import pallas as pl
from jax.experimental.pallas import tpu as pltpu

F32 = jnp.float32
BF16 = jnp.bfloat16

EPS = 1e-6
PAGE_SIZE = 128
WINDOW = 128
CONV_W = 3
NEG = -1e30
MASK_DIST = 1e9
LOG2E = 1.4426950408889634
LANES = 128
MXU_N = 256
V7X_VMEM_LIMIT = 56 * 1024 * 1024
ROW_TILE = 512
FFN_ROWS = 512
FFN_TILE = 512
HALO = 16
FOX_TQ = 512
PAGES_PER_CHUNK = 16
DECODE_SLOTS = 4
SWA_SAMPLE_ROWS = 8
SCAN_BLOCK = 256
SCAN_ROWS = 128


def _cparams(*sem):
    return pltpu.CompilerParams(dimension_semantics=sem, vmem_limit_bytes=V7X_VMEM_LIMIT)


def _rms(x):
    return x * lax.rsqrt(jnp.mean(x * x, axis=-1, keepdims=True) + EPS)


def _dot(a, b):
    return jnp.dot(a, b, preferred_element_type=F32)


def _dot_nt(a, b):
    return lax.dot_general(a, b, (((1,), (1,)), ((), ())), preferred_element_type=F32)


def _split3(x):
    hi = x.astype(BF16)
    r = x - hi.astype(F32)
    mid = r.astype(BF16)
    lo = (r - mid.astype(F32)).astype(BF16)
    return hi, mid, lo


def _const_spec(shape):
    nd = len(shape)
    return pl.BlockSpec(shape, lambda *_: (0,) * nd)


def _weight_spec(w, layer=0):
    return pl.BlockSpec((None,) + w.shape[1:], lambda *_: (layer, 0, 0), pipeline_mode=pl.Buffered(1))


def _fox_proj_kernel(x_ref, g_ref, w_ref, bf_ref, qg_ref, kg_ref,
                     q_ref, k_ref, v_ref, kb_ref, vb_ref, lf_ref, *, nq, nkv):
    a = (_rms(x_ref[...]) * g_ref[...]).astype(BF16)
    tm = a.shape[0]
    heads = nkv // LANES
    qg = qg_ref[...]
    kg = kg_ref[...]
    chunk = lambda lo: _dot_nt(a, w_ref[lo:lo + MXU_N, :].astype(BF16))
    for c in range(nq // MXU_N):
        z = chunk(c * MXU_N)
        for s in range(MXU_N // LANES):
            lo = c * MXU_N + s * LANES
            q_ref[:, lo:lo + LANES] = (_rms(z[:, s * LANES:(s + 1) * LANES]) * qg).astype(BF16)
    for c in range(nkv // MXU_N):
        z = chunk(nq + c * MXU_N)
        for s in range(MXU_N // LANES):
            lo = c * MXU_N + s * LANES
            kn = _rms(z[:, s * LANES:(s + 1) * LANES]) * kg
            k_ref[pl.ds(lo // LANES, tm, stride=heads), :] = kn
            kb_ref[:, lo:lo + LANES] = kn.astype(BF16)
    for c in range(nkv // MXU_N):
        z = chunk(nq + nkv + c * MXU_N)
        for s in range(MXU_N // LANES):
            lo = c * MXU_N + s * LANES
            v_ref[pl.ds(lo // LANES, tm, stride=heads), :] = z[:, s * LANES:(s + 1) * LANES]
        vb_ref[:, c * MXU_N:(c + 1) * MXU_N] = z.astype(BF16)
    zf = _dot_nt(a, w_ref[nq + 2 * nkv:, :].astype(BF16)) + bf_ref[...]
    lf_ref[...] = -(jnp.maximum(-zf, 0.0) + jnp.log1p(jnp.exp(-jnp.abs(zf))))


def _fox_proj(x, g, w_in, bf, qg, kg, nq, nkv):
    m, d = x.shape
    nh = w_in.shape[1] - nq - 2 * nkv
    heads = nkv // LANES
    tm = min(ROW_TILE, m)
    row = lambda n: pl.BlockSpec((tm, n), lambda i: (i, 0))
    kv_f32 = pl.BlockSpec((tm * heads, LANES), lambda i: (i, 0))
    return pl.pallas_call(
        functools.partial(_fox_proj_kernel, nq=nq, nkv=nkv),
        grid=(m // tm,),
        in_specs=[row(d), _const_spec(g.shape), _weight_spec(w_in), _const_spec(bf.shape), _const_spec(qg.shape),
                  _const_spec(kg.shape)],
        out_specs=[row(nq), kv_f32, kv_f32, row(nkv), row(nkv), row(nh)],
        out_shape=[jax.ShapeDtypeStruct((m, nq), BF16), jax.ShapeDtypeStruct((m * heads, LANES), F32),
                   jax.ShapeDtypeStruct((m * heads, LANES), F32), jax.ShapeDtypeStruct((m, nkv), BF16),
                   jax.ShapeDtypeStruct((m, nkv), BF16), jax.ShapeDtypeStruct((m, nh), F32)],
        compiler_params=_cparams("arbitrary"),
        name="fox_proj",
    )(x, g, w_in, bf, qg, kg)


def _scan_kernel(x_ref, tri_ref, o_ref, *, suffix):
    tr, n = x_ref.shape
    nblk = n // SCAN_BLOCK
    tri = tri_ref[...]
    carry = jnp.zeros((tr, 1), F32)
    order = range(nblk - 1, -1, -1) if suffix else range(nblk)
    for blk in order:
        cols = slice(blk * SCAN_BLOCK, (blk + 1) * SCAN_BLOCK)
        x = x_ref[:, cols]
        r = _dot(jnp.concatenate(_split3(x), axis=0), tri)
        y = r[0:tr] + r[tr:2 * tr] + r[2 * tr:3 * tr] + carry
        if suffix:
            o_ref[:, cols] = y
            carry = y[:, 0:1] + x[:, 0:1]
        else:
            o_ref[:, cols] = -y
            carry = y[:, SCAN_BLOCK - 1:SCAN_BLOCK]


def _scan_lanes(x, suffix):
    rows, n = x.shape
    tr = min(SCAN_ROWS, rows)
    i = lax.broadcasted_iota(jnp.int32, (SCAN_BLOCK, SCAN_BLOCK), 0)
    j = lax.broadcasted_iota(jnp.int32, (SCAN_BLOCK, SCAN_BLOCK), 1)
    tri = jnp.where(i > j if suffix else i <= j, 1.0, 0.0).astype(BF16)
    blk = pl.BlockSpec((tr, n), lambda i: (i, 0))
    return pl.pallas_call(
        functools.partial(_scan_kernel, suffix=suffix),
        grid=(rows // tr,),
        in_specs=[blk, _const_spec(tri.shape)],
        out_specs=blk,
        out_shape=jax.ShapeDtypeStruct((rows, n), F32),
        compiler_params=_cparams("arbitrary"),
        name="logf_suffix" if suffix else "logf_prefix",
    )(x, tri)


def _fox_prompt_kernel(q_ref, k_ref, v_ref, b_ref, o_ref, qs_ref, m_ref, l_ref, acc_ref, *, group, hd):
    i = pl.program_id(2)
    t = q_ref.shape[0]
    rows = group * t
    for g in range(group):
        qs_ref[g * t:(g + 1) * t, :] = q_ref[:, g * hd:(g + 1) * hd]
    m_ref[...] = jnp.full(m_ref.shape, NEG, F32)
    l_ref[...] = jnp.zeros(l_ref.shape, F32)
    acc_ref[...] = jnp.zeros(acc_ref.shape, F32)

    def tile(kt, masked):
        k0 = pl.multiple_of(kt * t, t)
        s = _dot_nt(qs_ref[...], k_ref[pl.ds(k0, t), :]).reshape(group, t, t)
        s = s + (b_ref[0, 0, :, pl.ds(k0, t)] * LOG2E)[:, None, :]
        if masked:
            r = lax.broadcasted_iota(jnp.int32, (group, t, t), 1)
            c = lax.broadcasted_iota(jnp.int32, (group, t, t), 2)
            s = jnp.where(c <= r, s, NEG)
        s = s.reshape(rows, t)
        m_old = m_ref[...]
        m_new = jnp.maximum(m_old, jnp.max(s, axis=-1, keepdims=True))
        p = jnp.exp2(s - jnp.concatenate([m_new] * (t // LANES), axis=1))
        alpha = jnp.exp2(m_old - m_new)
        l_ref[...] = alpha * l_ref[...] + jnp.sum(p, axis=-1, keepdims=True)
        acc_ref[...] = alpha * acc_ref[...] + _dot(p.astype(BF16), v_ref[pl.ds(k0, t), :])
        m_ref[...] = m_new

    def body(kt, carry):
        tile(kt, False)
        return carry

    lax.fori_loop(0, i, body, 0)
    tile(i, True)
    o = acc_ref[...] / l_ref[...]
    for g in range(group):
        o_ref[:, g * hd:(g + 1) * hd] = o[g * t:(g + 1) * t].astype(o_ref.dtype)


def _fox_prompt_attn(q, kb, vb, bias, batch, seq, nkv, hd):
    assert hd == LANES
    group = q.shape[1] // (nkv * hd)
    t = FOX_TQ
    nq = seq // t
    return pl.pallas_call(
        functools.partial(_fox_prompt_kernel, group=group, hd=hd),
        grid=(batch, nkv, nq),
        in_specs=[pl.BlockSpec((t, group * hd), lambda b, h, i: (b * nq + i, h)),
                  pl.BlockSpec((seq, hd), lambda b, h, i: (b, h)),
                  pl.BlockSpec((seq, hd), lambda b, h, i: (b, h)),
                  pl.BlockSpec((1, 1, group, seq), lambda b, h, i: (b, h, 0, 0))],
        out_specs=pl.BlockSpec((t, group * hd), lambda b, h, i: (b * nq + i, h)),
        out_shape=jax.ShapeDtypeStruct(q.shape, BF16),
        scratch_shapes=[pltpu.VMEM((group * t, hd), BF16), pltpu.VMEM((group * t, LANES), F32),
                        pltpu.VMEM((group * t, LANES), F32), pltpu.VMEM((group * t, hd), F32)],
        compiler_params=_cparams("arbitrary", "arbitrary", "arbitrary"),
        name="fox_prompt_attn",
    )(q, kb, vb, bias)


def _fox_sample_kernel(pt_ref, q_ref, kn_ref, vn_ref, bn_ref, e_ref, tri_ref, kpool, vpool, lpool,
                       o_ref, kbuf, vbuf, lbuf, sem, *, nkv, hd, n_chunks, rows_q, group):
    b = pl.program_id(0)
    nb = pl.num_programs(0)
    page_rows = PAGE_SIZE * nkv
    lc = PAGES_PER_CHUNK * PAGE_SIZE
    nh = lbuf.shape[2]
    pages_per_block = SCAN_BLOCK // PAGE_SIZE
    nblk = lc // SCAN_BLOCK

    def start_chunk(bb, step, slot):
        c = n_chunks - 1 - step
        for p in range(PAGES_PER_CHUNK):
            page = pt_ref[bb, c * PAGES_PER_CHUNK + p]
            src = pl.ds(pl.multiple_of(page * page_rows, page_rows), page_rows)
            dst = pl.ds(p * page_rows, page_rows)
            pltpu.make_async_copy(kpool.at[src, :], kbuf.at[slot, dst, :], sem.at[0, slot]).start()
            pltpu.make_async_copy(vpool.at[src, :], vbuf.at[slot, dst, :], sem.at[1, slot]).start()
            pltpu.make_async_copy(lpool.at[page], lbuf.at[slot, p], sem.at[2, slot]).start()

    def wait_chunk(slot):
        for buf, s in ((kbuf, 0), (vbuf, 1), (lbuf, 2)):
            pltpu.make_async_copy(buf.at[slot], buf.at[slot], sem.at[s, slot]).wait()

    ahead = DECODE_SLOTS - 1

    @pl.when(b == 0)
    def _():
        for g in range(ahead):
            start_chunk(g // n_chunks, g % n_chunks, g % DECODE_SLOTS)

    e = e_ref[...]
    tri = tri_ref[...]
    q = [q_ref[0, h] for h in range(nkv)]

    def chunk(step, carry):
        g = b * n_chunks + step
        slot = lax.rem(g, DECODE_SLOTS)
        nxt = g + ahead

        @pl.when(nxt < nb * n_chunks)
        def _():
            start_chunk(lax.div(nxt, n_chunks), lax.rem(nxt, n_chunks), lax.rem(nxt, DECODE_SLOTS))

        wait_chunk(slot)

        xs = [jnp.concatenate([lbuf[slot, k * pages_per_block + p] for p in range(pages_per_block)], axis=1)
              for k in range(nblk)]
        r = _dot(jnp.concatenate([part for x in xs for part in _split3(x)], axis=0), tri)
        sfx_carry = carry[nkv]
        ys = [None] * nblk
        for k in range(nblk - 1, -1, -1):
            local = r[(3 * k) * nh:(3 * k + 1) * nh] + r[(3 * k + 1) * nh:(3 * k + 2) * nh] \
                + r[(3 * k + 2) * nh:(3 * k + 3) * nh]
            ys[k] = local + sfx_carry
            sfx_carry = sfx_carry + (local[:, 0:1] + xs[k][:, 0:1])
        bias = _dot(e, jnp.concatenate(_split3(jnp.concatenate(ys, axis=1) * LOG2E), axis=0))
        new = []
        for h in range(nkv):
            m_old, l_old, acc_old = carry[h]
            k = kbuf[slot, pl.ds(h, lc, stride=nkv), :].astype(BF16)
            v = vbuf[slot, pl.ds(h, lc, stride=nkv), :].astype(BF16)
            s = _dot_nt(q[h], k) + bias[h * rows_q:(h + 1) * rows_q]
            m_new = jnp.maximum(m_old, jnp.max(s, axis=-1, keepdims=True))
            p = jnp.exp2(s - m_new)
            alpha = jnp.exp2(m_old - m_new)
            new.append((m_new, alpha * l_old + jnp.sum(p, axis=-1, keepdims=True),
                        alpha * acc_old + _dot(p.astype(BF16), v)))
        return tuple(new) + (sfx_carry,)

    init = tuple((jnp.full((rows_q, 1), NEG, F32), jnp.zeros((rows_q, 1), F32),
                  jnp.zeros((rows_q, hd), F32)) for _ in range(nkv)) + (jnp.zeros((nh, 1), F32),)
    state = lax.fori_loop(0, n_chunks, chunk, init)

    tn = kn_ref.shape[1]
    bias_n = _dot_nt(e, bn_ref[0])
    r = lax.broadcasted_iota(jnp.int32, (rows_q, tn), 0)
    c = lax.broadcasted_iota(jnp.int32, (rows_q, tn), 1)
    ok = c * group <= r
    for h in range(nkv):
        m_old, l_old, acc_old = state[h]
        k = kn_ref[0, :, h * hd:(h + 1) * hd].astype(BF16)
        v = vn_ref[0, :, h * hd:(h + 1) * hd].astype(BF16)
        s = jnp.where(ok, _dot_nt(q[h], k) + bias_n[h * rows_q:(h + 1) * rows_q], NEG)
        m_new = jnp.maximum(m_old, jnp.max(s, axis=-1, keepdims=True))
        p = jnp.exp2(s - m_new)
        alpha = jnp.exp2(m_old - m_new)
        l_new = alpha * l_old + jnp.sum(p, axis=-1, keepdims=True)
        acc = alpha * acc_old + _dot(p.astype(BF16), v)
        o_ref[0, h] = (acc / l_new).astype(o_ref.dtype)


def _fox_sample_attn(page_table, q, k_new, v_new, bias_new3, emat, kpool, vpool, lpool, nkv, hd, group):
    nb, _, rows_q, _ = q.shape
    n_pages = page_table.shape[1]
    nh = lpool.shape[1]
    n_chunks = n_pages // PAGES_PER_CHUNK
    buf_rows = PAGES_PER_CHUNK * PAGE_SIZE * nkv
    tn = k_new.shape[1]
    i = lax.broadcasted_iota(jnp.int32, (SCAN_BLOCK, SCAN_BLOCK), 0)
    j = lax.broadcasted_iota(jnp.int32, (SCAN_BLOCK, SCAN_BLOCK), 1)
    tri = jnp.where(i > j, 1.0, 0.0).astype(BF16)
    const = lambda a: pl.BlockSpec(a.shape, lambda b, pt: (0,) * a.ndim)
    grid_spec = pltpu.PrefetchScalarGridSpec(
        num_scalar_prefetch=1,
        grid=(nb,),
        in_specs=[pl.BlockSpec((1, nkv, rows_q, hd), lambda b, pt: (b, 0, 0, 0)),
                  pl.BlockSpec((1, tn, nkv * hd), lambda b, pt: (b, 0, 0)),
                  pl.BlockSpec((1, tn, nkv * hd), lambda b, pt: (b, 0, 0)),
                  pl.BlockSpec((1,) + bias_new3.shape[1:], lambda b, pt: (b, 0, 0)),
                  const(emat), const(tri),
                  pl.BlockSpec(memory_space=pl.ANY),
                  pl.BlockSpec(memory_space=pl.ANY),
                  pl.BlockSpec(memory_space=pl.ANY)],
        out_specs=pl.BlockSpec((1, nkv, rows_q, hd), lambda b, pt: (b, 0, 0, 0)),
        scratch_shapes=[pltpu.VMEM((DECODE_SLOTS, buf_rows, hd), F32), pltpu.VMEM((DECODE_SLOTS, buf_rows, hd), F32),
                        pltpu.VMEM((DECODE_SLOTS, PAGES_PER_CHUNK, nh, PAGE_SIZE), F32),
                        pltpu.SemaphoreType.DMA((3, DECODE_SLOTS))],
    )
    return pl.pallas_call(
        functools.partial(_fox_sample_kernel, nkv=nkv, hd=hd, n_chunks=n_chunks, rows_q=rows_q, group=group),
        grid_spec=grid_spec,
        out_shape=jax.ShapeDtypeStruct(q.shape, BF16),
        compiler_params=_cparams("arbitrary"),
        name="fox_sample_attn",
    )(page_table, q, k_new, v_new, bias_new3, emat, tri, kpool, vpool, lpool)


def _out_proj_kernel(o_ref, h_ref, w_ref, g_ref, h1_ref, c_ref):
    o = o_ref[...]
    tn = 512
    for c in range(h_ref.shape[1] // tn):
        cs = slice(c * tn, (c + 1) * tn)
        h1_ref[:, cs] = h_ref[:, cs] + _dot(o, w_ref[:, cs].astype(BF16))
    c_ref[...] = (_rms(h1_ref[...]) * g_ref[...]).astype(BF16)


def _out_proj(o, h, w, g):
    m, d = h.shape
    tm = min(ROW_TILE, m)
    row = lambda n: pl.BlockSpec((tm, n), lambda i: (i, 0))
    return pl.pallas_call(
        _out_proj_kernel,
        grid=(m // tm,),
        in_specs=[row(o.shape[1]), row(d), _weight_spec(w), _const_spec(g.shape)],
        out_specs=[row(d), row(d)],
        out_shape=[jax.ShapeDtypeStruct((m, d), F32), jax.ShapeDtypeStruct((m, d), BF16)],
        compiler_params=_cparams("arbitrary"),
        name="attn_out_proj",
    )(o, h, w, g)


def _ffn_begin(j, acc_ref):
    @pl.when(j == 0)
    def _():
        acc_ref[...] = jnp.zeros(acc_ref.shape, F32)


def _ffn_epilogue(j, h_ref, gn_ref, acc_ref, h2_ref, n_ref):
    @pl.when(j == pl.num_programs(1) - 1)
    def _():
        h2 = h_ref[...] + acc_ref[...]
        h2_ref[...] = h2
        n_ref[...] = (_rms(h2) * gn_ref[...]).astype(BF16)


def _ffn_prompt_kernel(c_ref, wg_ref, wv_ref, cwg_ref, cwv_ref, cbg_ref, cbv_ref, wo_ref, h_ref, gn_ref,
                       h2_ref, n_ref, tg_ref, tv_ref, acc_ref, ug_ref, uv_ref, pg_ref, pv_ref, *, tiles_per_seq):
    i, j = pl.program_id(0), pl.program_id(1)
    tm = c_ref.shape[0]
    first = lax.rem(i, tiles_per_seq) == 0
    _ffn_begin(j, acc_ref)

    @pl.when((i == 0) & (j == 0))
    def _():
        pg_ref[...] = jnp.zeros(pg_ref.shape, F32)
        pv_ref[...] = jnp.zeros(pv_ref.shape, F32)

    x = c_ref[...]
    for u_ref, p_ref, w_ref, t_ref in ((ug_ref, pg_ref, wg_ref, tg_ref), (uv_ref, pv_ref, wv_ref, tv_ref)):
        u_ref[0:HALO, :] = jnp.where(first, 0.0, p_ref[j])
        u_ref[HALO:, :] = _dot(x, w_ref[...])
        tail = u_ref[tm:tm + HALO, :]
        p_ref[j] = tail
        t_ref[...] = tail

    def conv(u_ref, cw_ref, cb_ref):
        y = cb_ref[...]
        for k in range(CONV_W):
            y = y + cw_ref[k:k + 1, :] * u_ref[pl.ds(HALO - (CONV_W - 1) + k, tm), :]
        return y

    act = (jax.nn.gelu(conv(ug_ref, cwg_ref, cbg_ref), approximate=True) * conv(uv_ref, cwv_ref, cbv_ref)).astype(BF16)
    acc_ref[...] += _dot(act, wo_ref[...].astype(BF16))
    _ffn_epilogue(j, h_ref, gn_ref, acc_ref, h2_ref, n_ref)


def _ffn_sample_kernel(c_ref, pg_ref, pv_ref, wg_ref, wv_ref, cwg_ref, cwv_ref, cbg_ref, cbv_ref, wo_ref,
                       h_ref, gn_ref, h2_ref, n_ref, tg_ref, tv_ref, acc_ref):
    j = pl.program_id(1)
    nb = pg_ref.shape[1]
    nt = c_ref.shape[0] // nb
    _ffn_begin(j, acc_ref)
    x = c_ref[...]

    def conv(u, p_ref, cw_ref, cb_ref, t_ref):
        slabs = [p_ref[k] for k in range(CONV_W - 1)] + [u[t * nb:(t + 1) * nb] for t in range(nt)]
        for k in range(CONV_W - 1):
            t_ref[k] = slabs[nt + k]
        y = cb_ref[...]
        for k in range(CONV_W):
            y = y + cw_ref[k:k + 1, :] * jnp.concatenate(slabs[k:k + nt], axis=0)
        return y

    g = conv(_dot(x, wg_ref[...]), pg_ref, cwg_ref, cbg_ref, tg_ref)
    v = conv(_dot(x, wv_ref[...]), pv_ref, cwv_ref, cbv_ref, tv_ref)
    act = (jax.nn.gelu(g, approximate=True) * v).astype(BF16)
    acc_ref[...] += _dot(act, wo_ref[...].astype(BF16))
    _ffn_epilogue(j, h_ref, gn_ref, acc_ref, h2_ref, n_ref)


def _conv_ffn(c, h, layer, w_in, conv_w, conv_b, w_out, g_next, past, tiles_per_seq):
    m, d = h.shape
    dff = w_out.shape[1]
    tf = FFN_TILE
    nf = dff // tf
    tm = min(FFN_ROWS, m)
    wspecs = [pl.BlockSpec((None, d, tf), lambda i, j: (layer, 0, j)),
              pl.BlockSpec((None, d, tf), lambda i, j: (layer, 0, nf + j)),
              pl.BlockSpec((None, CONV_W, tf), lambda i, j: (layer, 0, j)),
              pl.BlockSpec((None, CONV_W, tf), lambda i, j: (layer, 0, nf + j)),
              pl.BlockSpec((None, 1, tf), lambda i, j: (layer, 0, j)),
              pl.BlockSpec((None, 1, tf), lambda i, j: (layer, 0, nf + j)),
              pl.BlockSpec((None, tf, d), lambda i, j: (layer, j, 0))]
    wargs = [w_in, w_in, conv_w, conv_w, conv_b, conv_b, w_out]
    row = pl.BlockSpec((tm, d), lambda i, j: (i, 0))
    tail_specs = [row, pl.BlockSpec((1, d), lambda i, j: (0, 0))]
    if past is None:
        head_specs = [row]
        head_args = [c]
        kern = functools.partial(_ffn_prompt_kernel, tiles_per_seq=tiles_per_seq)
        scratch = [pltpu.VMEM((tm + HALO, tf), F32)] * 2 + [pltpu.VMEM((nf, HALO, tf), F32)] * 2
        state = jax.ShapeDtypeStruct((m // tm, HALO, dff), F32)
        state_spec = pl.BlockSpec((None, HALO, tf), lambda i, j: (i, 0, j))
    else:
        scratch = []
        nb = past.shape[2]
        head_specs = [row, pl.BlockSpec((None, CONV_W - 1, nb, tf), lambda i, j: (layer, 0, 0, j)),
                      pl.BlockSpec((None, CONV_W - 1, nb, tf), lambda i, j: (layer, 0, 0, nf + j))]
        head_args = [c, past, past]
        kern = _ffn_sample_kernel
        state = jax.ShapeDtypeStruct((CONV_W - 1, nb, dff), F32)
        state_spec = pl.BlockSpec((CONV_W - 1, nb, tf), lambda i, j: (0, 0, j))
    return pl.pallas_call(
        kern,
        grid=(m // tm, nf),
        in_specs=head_specs + wspecs + tail_specs,
        out_specs=[row, row, state_spec, state_spec],
        out_shape=[jax.ShapeDtypeStruct((m, d), F32), jax.ShapeDtypeStruct((m, d), BF16), state, state],
        scratch_shapes=[pltpu.VMEM((tm, d), F32)] + scratch,
        compiler_params=_cparams("arbitrary", "arbitrary"),
        name="conv_ffn_prompt" if past is None else "conv_ffn_sample",
    )(*head_args, *wargs, h, g_next)


def _ple_kernel(h_ref, n_ref, p_ref, wp_ref, wg_ref, *rest, n_norms):
    g_refs = rest[:n_norms]
    h3_ref = rest[n_norms]
    a_refs = rest[n_norms + 1:]
    n = n_ref[...]
    p = p_ref[...].astype(BF16)
    d = h_ref.shape[1]
    tn = 512
    for c in range(d // tn):
        cs = slice(c * tn, (c + 1) * tn)
        gate = jax.nn.sigmoid(_dot(n, wg_ref[:, cs].astype(BF16)))
        h3_ref[:, cs] = h_ref[:, cs] + _dot(p, wp_ref[:, cs].astype(BF16)) * gate
    if n_norms:
        xhat = _rms(h3_ref[...])
        for g_ref, a_ref in zip(g_refs, a_refs):
            a_ref[...] = (xhat * g_ref[...]).astype(BF16)


def _ple(h, n, p, layer, wp, wg, gains):
    m, d = h.shape
    tm = min(ROW_TILE, m)
    row = lambda w: pl.BlockSpec((tm, w), lambda i: (i, 0))
    k = len(gains)
    outs = pl.pallas_call(
        functools.partial(_ple_kernel, n_norms=k),
        grid=(m // tm,),
        in_specs=[row(d), row(d), pl.BlockSpec((None, tm, p.shape[2]), lambda i: (layer, i, 0)),
                  _weight_spec(wp, layer), _weight_spec(wg, layer)]
                 + [_const_spec(g.shape) for g in gains],
        out_specs=[row(d)] * (k + 1),
        out_shape=[jax.ShapeDtypeStruct((m, d), F32)] + [jax.ShapeDtypeStruct((m, d), BF16)] * k,
        compiler_params=_cparams("arbitrary"),
        name="ple",
    )(h, n, p, wp, wg, *gains)
    return outs


def _swa_proj_kernel(a_ref, akv_ref, wq_ref, wkv_ref, qg_ref, kg_ref, q_ref, k_ref, v_ref, *, nkv, hd):
    a = a_ref[...]
    qg = qg_ref[...]
    lane = lax.broadcasted_iota(jnp.int32, (a.shape[0], LANES), 1)
    low = lane < hd
    for c in range(wq_ref.shape[1] // MXU_N):
        z = _dot(a, wq_ref[:, c * MXU_N:(c + 1) * MXU_N].astype(BF16))
        for s in range(MXU_N // LANES):
            zz = z[:, s * LANES:(s + 1) * LANES]
            sq = zz * zz
            s_lo = jnp.sum(jnp.where(low, sq, 0.0), axis=-1, keepdims=True)
            s_hi = jnp.sum(jnp.where(low, 0.0, sq), axis=-1, keepdims=True)
            inv = jnp.where(low, lax.rsqrt(s_lo / hd + EPS), lax.rsqrt(s_hi / hd + EPS))
            lo = c * MXU_N + s * LANES
            q_ref[:, lo:lo + LANES] = (zz * inv * qg).astype(BF16)
    akv = akv_ref[...]
    kg = kg_ref[...]
    for c in range(nkv // MXU_N):
        z = _dot(akv, wkv_ref[:, c * MXU_N:(c + 1) * MXU_N])
        for s in range(MXU_N // LANES):
            lo = c * MXU_N + s * LANES
            k_ref[:, lo:lo + LANES] = _rms(z[:, s * LANES:(s + 1) * LANES]) * kg
    for c in range(nkv // MXU_N):
        v_ref[:, c * MXU_N:(c + 1) * MXU_N] = _dot(akv, wkv_ref[:, nkv + c * MXU_N:nkv + (c + 1) * MXU_N])


def _swa_proj(a, akv, wq, wkv_dup, qg, kg, hd):
    m, d = a.shape
    nkv = wkv_dup.shape[2] // 2
    nq = wq.shape[2]
    tm = min(ROW_TILE, m)
    row = lambda n: pl.BlockSpec((tm, n), lambda i: (i, 0))
    return pl.pallas_call(
        functools.partial(_swa_proj_kernel, nkv=nkv, hd=hd),
        grid=(m // tm,),
        in_specs=[row(d), row(d), _weight_spec(wq), _weight_spec(wkv_dup),
                  _const_spec(qg.shape), _const_spec(kg.shape)],
        out_specs=[row(nq), row(nkv), row(nkv)],
        out_shape=[jax.ShapeDtypeStruct((m, nq), BF16), jax.ShapeDtypeStruct((m, nkv), F32),
                   jax.ShapeDtypeStruct((m, nkv), F32)],
        compiler_params=_cparams("arbitrary"),
        name="swa_proj",
    )(a, akv, wq, wkv_dup, qg, kg)


def _swa_prompt_kernel(slope_ref, sink_ref, q_ref, kp_ref, kc_ref, vp_ref, vc_ref, o_ref, *, nkv, hd):
    i = pl.program_id(1)
    w = q_ref.shape[0]
    pairs = q_ref.shape[1] // (nkv * LANES)
    lane = lax.broadcasted_iota(jnp.int32, (w, LANES), 1)
    low = lane < hd
    r = lax.broadcasted_iota(jnp.int32, (w, 2 * w), 0)
    c = lax.broadcasted_iota(jnp.int32, (w, 2 * w), 1)
    dist = r + w - c
    valid = (dist >= 0) & (dist < w) & ((c >= w) | (i > 0))
    base = jnp.where(valid, dist.astype(F32), MASK_DIST)
    top = lax.broadcasted_iota(jnp.int32, (2 * w, 1), 0) < w
    for h in range(nkv):
        k = jnp.concatenate([kp_ref[:, h * LANES:(h + 1) * LANES], kc_ref[:, h * LANES:(h + 1) * LANES]],
                            axis=0).astype(BF16)
        v = jnp.concatenate([vp_ref[:, h * LANES:(h + 1) * LANES], vc_ref[:, h * LANES:(h + 1) * LANES]],
                            axis=0).astype(BF16)
        for pr in range(pairs):
            col = (h * pairs + pr) * LANES
            h0 = 2 * (h * pairs + pr)
            qp = q_ref[:, col:col + LANES]
            rows = jnp.concatenate([jnp.where(low, qp, jnp.zeros_like(qp)),
                                    jnp.where(low, jnp.zeros_like(qp), qp)], axis=0)
            s = _dot_nt(rows, k) - jnp.concatenate([slope_ref[h0] * base, slope_ref[h0 + 1] * base], axis=0)
            sink = jnp.where(top, sink_ref[h0], sink_ref[h0 + 1])
            m = jnp.maximum(jnp.max(s, axis=-1, keepdims=True), sink)
            p = jnp.exp2(s - m)
            den = jnp.sum(p, axis=-1, keepdims=True) + jnp.exp2(sink - m)
            o = _dot(p.astype(BF16), v) / den
            o_ref[:, col:col + LANES] = jnp.where(low, o[:w], o[w:]).astype(o_ref.dtype)


def _swa_prompt_attn(q, kdup, vdup, slopes, sinks, batch, seq, nkv, hd):
    nblk = seq // WINDOW
    dq = q.shape[1]
    dk = kdup.shape[1]
    cur = lambda b, i: (b * nblk + i, 0)
    prev = lambda b, i: (b * nblk + jnp.maximum(i - 1, 0), 0)
    smem = pl.BlockSpec(memory_space=pltpu.SMEM)
    return pl.pallas_call(
        functools.partial(_swa_prompt_kernel, nkv=nkv, hd=hd),
        grid=(batch, nblk),
        in_specs=[smem, smem, pl.BlockSpec((WINDOW, dq), cur), pl.BlockSpec((WINDOW, dk), prev),
                  pl.BlockSpec((WINDOW, dk), cur), pl.BlockSpec((WINDOW, dk), prev),
                  pl.BlockSpec((WINDOW, dk), cur)],
        out_specs=pl.BlockSpec((WINDOW, dq), cur),
        out_shape=jax.ShapeDtypeStruct(q.shape, BF16),
        compiler_params=_cparams("arbitrary", "arbitrary"),
        name="swa_prompt_attn",
    )(slopes, sinks, q, kdup, kdup, vdup, vdup)


def _swa_sample_kernel(q_ref, kc_ref, vc_ref, kn_ref, vn_ref, slope_ref, sink_ref, place_ref, o_ref, wk_ref, wv_ref,
                       *, heads, t_new):
    rows = q_ref.shape[1]
    w = kc_ref.shape[2]
    tn = kn_ref.shape[1]
    slope = slope_ref[...]
    sink = sink_ref[...]
    place = place_ref[...]
    is_new = lax.broadcasted_iota(jnp.int32, kc_ref.shape[1:], 1) >= w - t_new
    t_p = lax.broadcasted_iota(jnp.int32, (rows, w), 0) // heads
    j_p = lax.broadcasted_iota(jnp.int32, (rows, w), 1)
    dist_p = t_p + w - j_p
    bias_p = jnp.where(dist_p < WINDOW, slope * dist_p.astype(F32), -NEG)
    t_n = lax.broadcasted_iota(jnp.int32, (rows, tn), 0) // heads
    j_n = lax.broadcasted_iota(jnp.int32, (rows, tn), 1)
    dist_n = t_n - j_n
    bias_n = jnp.where(dist_n >= 0, slope * dist_n.astype(F32), -NEG)
    for b in range(q_ref.shape[0]):
        q = q_ref[b]
        s_p = _dot(q, kc_ref[b].astype(BF16)) - bias_p
        s_n = _dot_nt(q, kn_ref[b].astype(BF16)) - bias_n
        m = jnp.maximum(jnp.maximum(jnp.max(s_p, axis=-1, keepdims=True), jnp.max(s_n, axis=-1, keepdims=True)),
                        sink)
        p_p = jnp.exp2(s_p - m)
        p_n = jnp.exp2(s_n - m)
        den = jnp.sum(p_p, axis=-1, keepdims=True) + jnp.sum(p_n, axis=-1, keepdims=True) + jnp.exp2(sink - m)
        o = _dot_nt(p_p.astype(BF16), vc_ref[b].astype(BF16)) + _dot(p_n.astype(BF16), vn_ref[b].astype(BF16))
        o_ref[b] = o / den
        for c_ref, n_ref, w_ref in ((kc_ref, kn_ref, wk_ref), (vc_ref, vn_ref, wv_ref)):
            new_t = sum(_dot(place, part) for part in _split3(n_ref[b]))
            w_ref[b] = jnp.where(is_new, new_t.T, pltpu.roll(c_ref[b], w - t_new, axis=1))


def _swa_sample_attn(q, kct, vct, kn, vn, slope_col, sink_col, heads, t_new):
    nb, rows, dk = q.shape
    w = kct.shape[2]
    g = SWA_SAMPLE_ROWS
    place = (jnp.arange(w)[:, None] == w - t_new + jnp.arange(kn.shape[1])[None, :]).astype(BF16)
    blk = lambda a: pl.BlockSpec((g,) + a.shape[1:], lambda b: (b, 0, 0))
    return pl.pallas_call(
        functools.partial(_swa_sample_kernel, heads=heads, t_new=t_new),
        grid=(nb // g,),
        in_specs=[blk(q), blk(kct), blk(vct), blk(kn), blk(vn), _const_spec(slope_col.shape),
                  _const_spec(sink_col.shape), _const_spec(place.shape)],
        out_specs=[blk(q), blk(kct), blk(vct)],
        out_shape=[jax.ShapeDtypeStruct((nb, rows, dk), F32), jax.ShapeDtypeStruct(kct.shape, F32),
                   jax.ShapeDtypeStruct(vct.shape, F32)],
        compiler_params=_cparams("arbitrary"),
        name="swa_sample_attn",
    )(q, kct, vct, kn, vn, slope_col, sink_col, place)


def kernel(x_prompt, x_sample, cache_fox_k, cache_fox_v, cache_fox_logf, cache_win_k, cache_win_v, state_conv, page_table, p_prompt, p_sample, norm_attn_g, norm_ffn_g, norm_ple_g, fox_w_in, fox_b_f, fox_q_norm_g, fox_k_norm_g, fox_w_out, kv_norm_g, swa_w_kv, swa_k_norm_g, swa_w_q, swa_q_norm_g, swa_sinks, swa_w_out, ffn_w_in, ffn_conv_w, ffn_conv_b, ffn_w_out, ple_w_proj, ple_w_gate):
    B, S, D = x_prompt.shape
    NB, T, _ = x_sample.shape
    HD_A = fox_q_norm_g.shape[-1]
    H_A = fox_b_f.shape[-1]
    KVH_A = cache_fox_k.shape[3]
    G_A = H_A // KVH_A
    HD_B = swa_q_norm_g.shape[-1]
    H_B = swa_sinks.shape[-1]
    KVH_B = cache_win_k.shape[2]
    G_B = H_B // KVH_B
    WIN = cache_win_k.shape[1]
    L = page_table.shape[1] * PAGE_SIZE
    DFF = ffn_w_out.shape[1]
    NQ_A, NK_A = H_A * HD_A, KVH_A * HD_A
    row2 = lambda g: g.reshape(1, -1)

    bf_a = fox_b_f
    qg_a = row2(fox_q_norm_g[0] * (HD_A ** -0.5 * LOG2E))
    kg_a = row2(fox_k_norm_g[0])
    dup = lambda w: jnp.repeat(w.reshape(D, KVH_B, 1, HD_B), LANES // HD_B, axis=2).reshape(D, KVH_B * LANES)
    wk_b, wv_b = jnp.split(swa_w_kv, 2, axis=-1)
    wkv_b = jnp.concatenate([dup(wk_b), dup(wv_b)], axis=1).astype(BF16)[None]
    qg_b = row2(jnp.tile(swa_q_norm_g[0] * (HD_B ** -0.5 * LOG2E), LANES // HD_B))
    kg_b = row2(jnp.tile(swa_k_norm_g, LANES // HD_B))
    ffn_wi = ffn_w_in.astype(BF16)
    ffn_wo = ffn_w_out
    slopes = jnp.exp2(-8.0 * jnp.arange(1, H_B + 1, dtype=F32) / H_B) * LOG2E
    sinks = swa_sinks[0] * LOG2E

    xp = x_prompt.reshape(B * S, D)
    xs = x_sample.transpose(1, 0, 2).reshape(T * NB, D)
    pp = p_prompt.reshape(2, B * S, -1)
    ps = p_sample.transpose(0, 2, 1, 3).reshape(2, T * NB, -1)
    past = state_conv.transpose(0, 2, 1, 3)

    g0 = row2(norm_attn_g[0])
    w_in_t = fox_w_in.transpose(0, 2, 1)
    q_p, k_p, v_p, kb_p, vb_p, lf_p = _fox_proj(xp, g0, w_in_t, bf_a, qg_a, kg_a, NQ_A, NK_A)
    q_s, k_s, v_s, _, _, lf_s = _fox_proj(xs, g0, w_in_t, bf_a, qg_a, kg_a, NQ_A, NK_A)

    bias_p = _scan_lanes(lf_p.reshape(B, S, H_A).transpose(0, 2, 1).reshape(B * H_A, S), suffix=False)
    o_p = _fox_prompt_attn(q_p, kb_p, vb_p, bias_p.reshape(B, KVH_A, G_A, S), B, S, KVH_A, HD_A)

    tn = 16
    lf_new = lf_s.reshape(T, NB, H_A).transpose(1, 2, 0).reshape(NB * H_A, T)
    bias_new = _scan_lanes(jnp.pad(lf_new, ((0, 0), (0, SCAN_BLOCK - T))), suffix=False)[:, :tn]
    bias_new3 = jnp.concatenate(_split3(bias_new.reshape(NB, H_A, tn).transpose(0, 2, 1) * LOG2E), axis=-1)
    rows_q = T * G_A
    head_of_row = (jnp.arange(KVH_A)[:, None] * G_A + jnp.arange(rows_q)[None, :] % G_A).reshape(-1)
    emat = jnp.tile(jax.nn.one_hot(head_of_row, H_A, dtype=BF16), (1, 3))
    qs_b = q_s.reshape(T, NB, KVH_A, G_A, HD_A).transpose(1, 2, 0, 3, 4).reshape(NB, KVH_A, rows_q, HD_A)
    pad_new = lambda a: jnp.pad(a.reshape(T, NB, NK_A).transpose(1, 0, 2), ((0, 0), (0, tn - T), (0, 0)))
    kpool = cache_fox_k.reshape(-1, HD_A)
    vpool = cache_fox_v.reshape(-1, HD_A)
    o_s = _fox_sample_attn(page_table, qs_b, pad_new(k_s), pad_new(v_s), bias_new3, emat, kpool, vpool,
                           cache_fox_logf.reshape(cache_fox_logf.shape[1:]).transpose(0, 2, 1), KVH_A, HD_A, G_A)
    o_s = o_s.reshape(NB, KVH_A, T, G_A, HD_A).transpose(2, 0, 1, 3, 4).reshape(T * NB, NQ_A)

    gf0 = row2(norm_ffn_g[0])
    h_p, c_p = _out_proj(o_p, xp, fox_w_out, gf0)
    h_s, c_s = _out_proj(o_s, xs, fox_w_out, gf0)

    gp0 = row2(norm_ple_g[0])
    tiles_per_seq = S // FFN_ROWS
    conv_b = ffn_conv_b[:, None, :]
    h_p, n_p, *state_p0 = _conv_ffn(c_p, h_p, 0, ffn_wi, ffn_conv_w, conv_b, ffn_wo, gp0, None, tiles_per_seq)
    h_s, n_s, *state_s0 = _conv_ffn(c_s, h_s, 0, ffn_wi, ffn_conv_w, conv_b, ffn_wo, gp0, past, 1)

    gq1, gkv = row2(norm_attn_g[1]), row2(kv_norm_g)
    h_p, a_p, akv_p = _ple(h_p, n_p, pp, 0, ple_w_proj, ple_w_gate, [gq1, gkv])
    h_s, a_s, akv_s = _ple(h_s, n_s, ps, 0, ple_w_proj, ple_w_gate, [gq1, gkv])

    q1_p, kd_p, vd_p = _swa_proj(a_p, akv_p, swa_w_q, wkv_b, qg_b, kg_b, HD_B)
    q1_s, kd_s, vd_s = _swa_proj(a_s, akv_s, swa_w_q, wkv_b, qg_b, kg_b, HD_B)
    undup = lambda a: a.reshape(a.shape[0], KVH_B, LANES // HD_B, HD_B)[:, :, 0]
    o1_p = _swa_prompt_attn(q1_p, kd_p, vd_p, slopes, sinks, B, S, KVH_B, HD_B)

    ksh_s = undup(kd_s).reshape(T, NB, KVH_B * HD_B).transpose(1, 0, 2)
    vsh_s = undup(vd_s).reshape(T, NB, KVH_B * HD_B).transpose(1, 0, 2)
    q1h = q1_s.reshape(T, NB, H_B, 1, HD_B).transpose(1, 0, 2, 3, 4)
    slab = (jnp.arange(H_B)[:, None] // G_B == jnp.arange(KVH_B)[None, :])[None, None, :, :, None]
    q1x = jnp.where(slab, q1h, jnp.zeros((), BF16)).reshape(NB, T * H_B, KVH_B * HD_B)
    padn = lambda a: jnp.pad(a, ((0, 0), (0, tn - T), (0, 0)))
    slope_col = jnp.tile(slopes, T).reshape(T * H_B, 1)
    sink_col = jnp.tile(sinks, T).reshape(T * H_B, 1)
    feature_major = lambda a: a.transpose(0, 2, 3, 1).reshape(NB, KVH_B * HD_B, WIN)
    o1x, wk_s, wv_s = _swa_sample_attn(q1x, feature_major(cache_win_k), feature_major(cache_win_v),
                                       padn(ksh_s), padn(vsh_s), slope_col, sink_col, H_B, T)
    o1x = o1x.reshape(NB, T, H_B, KVH_B, HD_B)
    o1_s = jnp.sum(jnp.where(slab, o1x, 0.0), axis=3)
    o1_s = o1_s.transpose(1, 0, 2, 3).reshape(T * NB, H_B * HD_B).astype(BF16)

    gf1 = row2(norm_ffn_g[1])
    h_p, c_p = _out_proj(o1_p, h_p, swa_w_out, gf1)
    h_s, c_s = _out_proj(o1_s, h_s, swa_w_out, gf1)

    gp1 = row2(norm_ple_g[1])
    h_p, n_p, *state_p1 = _conv_ffn(c_p, h_p, 1, ffn_wi, ffn_conv_w, conv_b, ffn_wo, gp1, None, tiles_per_seq)
    h_s, n_s, *state_s1 = _conv_ffn(c_s, h_s, 1, ffn_wi, ffn_conv_w, conv_b, ffn_wo, gp1, past, 1)

    (y_p,) = _ple(h_p, n_p, pp, 1, ple_w_proj, ple_w_gate, [])
    (y_s,) = _ple(h_s, n_s, ps, 1, ple_w_proj, ple_w_gate, [])

    to_bt = lambda a, *tail: a.reshape((T, NB) + tail).swapaxes(0, 1)
    y_prompt = y_p.reshape(B, S, D)
    y_sample = to_bt(y_s, D)
    fk_p = k_p.reshape(1, B, S, KVH_A, HD_A)
    fv_p = v_p.reshape(1, B, S, KVH_A, HD_A)
    flf_p = lf_p.reshape(1, B, S, H_A)
    last_win = lambda a: undup(a.reshape(B, S, -1)[:, S - WIN:].reshape(B * WIN, -1)).reshape(B, WIN, KVH_B, HD_B)
    win_k_p = last_win(kd_p)
    win_v_p = last_win(vd_p)
    conv_p = jnp.stack([jnp.concatenate([t[tiles_per_seq - 1::tiles_per_seq, HALO - (CONV_W - 1):] for t in st],
                                        axis=-1) for st in (state_p0, state_p1)])
    conv_s = jnp.stack([jnp.concatenate(st, axis=-1).swapaxes(0, 1) for st in (state_s0, state_s1)])
    fk_s = to_bt(k_s, KVH_A, HD_A)[None]
    fv_s = to_bt(v_s, KVH_A, HD_A)[None]
    flf_s = to_bt(lf_s, H_A)[None]
    position_major = lambda a: a.reshape(NB, KVH_B, HD_B, WIN).transpose(0, 3, 1, 2)
    win_k_s = position_major(wk_s)
    win_v_s = position_major(wv_s)
    return (y_prompt, y_sample, fk_p, fv_p, flf_p, win_k_p, win_v_p, conv_p,
            fk_s, fv_s, flf_s, win_k_s, win_v_s, conv_s)
```

```python
import functools

import jax
import jax.numpy as jnp
from jax import lax
from jax.experimental import pallas as pl
from jax.experimental.pallas import tpu as pltpu

F32 = jnp.float32
BF16 = jnp.bfloat16

EPS = 1e-6
PAGE_SIZE = 128
WINDOW = 128
CONV_W = 3
NEG = -1e30
MASK_DIST = 1e9
LOG2E = 1.4426950408889634
LANES = 128
MXU_N = 256
V7X_VMEM_LIMIT = 56 * 1024 * 1024
ROW_TILE = 512
FFN_ROWS = 512
FFN_TILE = 512
HALO = 16
FOX_TQ = 512
PAGES_PER_CHUNK = 16
DECODE_SLOTS = 3
SWA_SAMPLE_ROWS = 8
SCAN_BLOCK = 256
SCAN_ROWS = 128


def _cparams(*sem):
    return pltpu.CompilerParams(dimension_semantics=sem, vmem_limit_bytes=V7X_VMEM_LIMIT)


def _rms(x):
    return x * lax.rsqrt(jnp.mean(x * x, axis=-1, keepdims=True) + EPS)


def _dot(a, b):
    return jnp.dot(a, b, preferred_element_type=F32)


def _dot_nt(a, b):
    return lax.dot_general(a, b, (((1,), (1,)), ((), ())), preferred_element_type=F32)


def _split3(x):
    hi = x.astype(BF16)
    r = x - hi.astype(F32)
    mid = r.astype(BF16)
    lo = (r - mid.astype(F32)).astype(BF16)
    return hi, mid, lo


def _const_spec(shape):
    nd = len(shape)
    return pl.BlockSpec(shape, lambda *_: (0,) * nd)


def _weight_spec(w, layer=0):
    return pl.BlockSpec((None,) + w.shape[1:], lambda *_: (layer, 0, 0), pipeline_mode=pl.Buffered(1))


def _fox_proj_kernel(x_ref, g_ref, w_ref, bf_ref, qg_ref, kg_ref,
                     q_ref, k_ref, v_ref, kb_ref, vb_ref, lf_ref, *, nq, nkv):
    a = (_rms(x_ref[...]) * g_ref[...]).astype(BF16)
    tm = a.shape[0]
    heads = nkv // LANES
    qg = qg_ref[...]
    kg = kg_ref[...]
    chunk = lambda lo: _dot_nt(a, w_ref[lo:lo + MXU_N, :].astype(BF16))
    for c in range(nq // MXU_N):
        z = chunk(c * MXU_N)
        for s in range(MXU_N // LANES):
            lo = c * MXU_N + s * LANES
            q_ref[:, lo:lo + LANES] = (_rms(z[:, s * LANES:(s + 1) * LANES]) * qg).astype(BF16)
    for c in range(nkv // MXU_N):
        z = chunk(nq + c * MXU_N)
        for s in range(MXU_N // LANES):
            lo = c * MXU_N + s * LANES
            kn = _rms(z[:, s * LANES:(s + 1) * LANES]) * kg
            k_ref[pl.ds(lo // LANES, tm, stride=heads), :] = kn
            kb_ref[:, lo:lo + LANES] = kn.astype(BF16)
    for c in range(nkv // MXU_N):
        z = chunk(nq + nkv + c * MXU_N)
        for s in range(MXU_N // LANES):
            lo = c * MXU_N + s * LANES
            v_ref[pl.ds(lo // LANES, tm, stride=heads), :] = z[:, s * LANES:(s + 1) * LANES]
        vb_ref[:, c * MXU_N:(c + 1) * MXU_N] = z.astype(BF16)
    zf = _dot_nt(a, w_ref[nq + 2 * nkv:, :].astype(BF16)) + bf_ref[...]
    lf_ref[...] = -(jnp.maximum(-zf, 0.0) + jnp.log1p(jnp.exp(-jnp.abs(zf))))


def _fox_proj(x, g, w_in, bf, qg, kg, nq, nkv):
    m, d = x.shape
    nh = w_in.shape[1] - nq - 2 * nkv
    heads = nkv // LANES
    tm = min(ROW_TILE, m)
    row = lambda n: pl.BlockSpec((tm, n), lambda i: (i, 0))
    kv_f32 = pl.BlockSpec((tm * heads, LANES), lambda i: (i, 0))
    return pl.pallas_call(
        functools.partial(_fox_proj_kernel, nq=nq, nkv=nkv),
        grid=(m // tm,),
        in_specs=[row(d), _const_spec(g.shape), _weight_spec(w_in), _const_spec(bf.shape), _const_spec(qg.shape),
                  _const_spec(kg.shape)],
        out_specs=[row(nq), kv_f32, kv_f32, row(nkv), row(nkv), row(nh)],
        out_shape=[jax.ShapeDtypeStruct((m, nq), BF16), jax.ShapeDtypeStruct((m * heads, LANES), F32),
                   jax.ShapeDtypeStruct((m * heads, LANES), F32), jax.ShapeDtypeStruct((m, nkv), BF16),
                   jax.ShapeDtypeStruct((m, nkv), BF16), jax.ShapeDtypeStruct((m, nh), F32)],
        compiler_params=_cparams("arbitrary"),
        name="fox_proj",
    )(x, g, w_in, bf, qg, kg)


def _scan_kernel(x_ref, tri_ref, o_ref, *, suffix):
    tr, n = x_ref.shape
    nblk = n // SCAN_BLOCK
    tri = tri_ref[...]
    carry = jnp.zeros((tr, 1), F32)
    order = range(nblk - 1, -1, -1) if suffix else range(nblk)
    for blk in order:
        cols = slice(blk * SCAN_BLOCK, (blk + 1) * SCAN_BLOCK)
        x = x_ref[:, cols]
        r = _dot(jnp.concatenate(_split3(x), axis=0), tri)
        y = r[0:tr] + r[tr:2 * tr] + r[2 * tr:3 * tr] + carry
        if suffix:
            o_ref[:, cols] = y
            carry = y[:, 0:1] + x[:, 0:1]
        else:
            o_ref[:, cols] = -y
            carry = y[:, SCAN_BLOCK - 1:SCAN_BLOCK]


def _scan_lanes(x, suffix):
    rows, n = x.shape
    tr = min(SCAN_ROWS, rows)
    i = lax.broadcasted_iota(jnp.int32, (SCAN_BLOCK, SCAN_BLOCK), 0)
    j = lax.broadcasted_iota(jnp.int32, (SCAN_BLOCK, SCAN_BLOCK), 1)
    tri = jnp.where(i > j if suffix else i <= j, 1.0, 0.0).astype(BF16)
    blk = pl.BlockSpec((tr, n), lambda i: (i, 0))
    return pl.pallas_call(
        functools.partial(_scan_kernel, suffix=suffix),
        grid=(rows // tr,),
        in_specs=[blk, _const_spec(tri.shape)],
        out_specs=blk,
        out_shape=jax.ShapeDtypeStruct((rows, n), F32),
        compiler_params=_cparams("arbitrary"),
        name="logf_suffix" if suffix else "logf_prefix",
    )(x, tri)


def _fox_prompt_kernel(q_ref, k_ref, v_ref, b_ref, o_ref, qs_ref, m_ref, l_ref, acc_ref, *, group, hd):
    i = pl.program_id(2)
    t = q_ref.shape[0]
    rows = group * t
    for g in range(group):
        qs_ref[g * t:(g + 1) * t, :] = q_ref[:, g * hd:(g + 1) * hd]
    m_ref[...] = jnp.full(m_ref.shape, NEG, F32)
    l_ref[...] = jnp.zeros(l_ref.shape, F32)
    acc_ref[...] = jnp.zeros(acc_ref.shape, F32)

    def tile(kt, masked):
        k0 = pl.multiple_of(kt * t, t)
        s = _dot_nt(qs_ref[...], k_ref[pl.ds(k0, t), :]).reshape(group, t, t)
        s = s + (b_ref[0, 0, :, pl.ds(k0, t)] * LOG2E)[:, None, :]
        if masked:
            r = lax.broadcasted_iota(jnp.int32, (group, t, t), 1)
            c = lax.broadcasted_iota(jnp.int32, (group, t, t), 2)
            s = jnp.where(c <= r, s, NEG)
        s = s.reshape(rows, t)
        m_old = m_ref[...]
        m_new = jnp.maximum(m_old, jnp.max(s, axis=-1, keepdims=True))
        p = jnp.exp2(s - jnp.concatenate([m_new] * (t // LANES), axis=1))
        alpha = jnp.exp2(m_old - m_new)
        l_ref[...] = alpha * l_ref[...] + jnp.sum(p, axis=-1, keepdims=True)
        acc_ref[...] = alpha * acc_ref[...] + _dot(p.astype(BF16), v_ref[pl.ds(k0, t), :])
        m_ref[...] = m_new

    def body(kt, carry):
        tile(kt, False)
        return carry

    lax.fori_loop(0, i, body, 0)
    tile(i, True)
    o = acc_ref[...] / l_ref[...]
    for g in range(group):
        o_ref[:, g * hd:(g + 1) * hd] = o[g * t:(g + 1) * t].astype(o_ref.dtype)


def _fox_prompt_attn(q, kb, vb, bias, batch, seq, nkv, hd):
    assert hd == LANES
    group = q.shape[1] // (nkv * hd)
    t = FOX_TQ
    nq = seq // t
    return pl.pallas_call(
        functools.partial(_fox_prompt_kernel, group=group, hd=hd),
        grid=(batch, nkv, nq),
        in_specs=[pl.BlockSpec((t, group * hd), lambda b, h, i: (b * nq + i, h)),
                  pl.BlockSpec((seq, hd), lambda b, h, i: (b, h)),
                  pl.BlockSpec((seq, hd), lambda b, h, i: (b, h)),
                  pl.BlockSpec((1, 1, group, seq), lambda b, h, i: (b, h, 0, 0))],
        out_specs=pl.BlockSpec((t, group * hd), lambda b, h, i: (b * nq + i, h)),
        out_shape=jax.ShapeDtypeStruct(q.shape, BF16),
        scratch_shapes=[pltpu.VMEM((group * t, hd), BF16), pltpu.VMEM((group * t, LANES), F32),
                        pltpu.VMEM((group * t, LANES), F32), pltpu.VMEM((group * t, hd), F32)],
        compiler_params=_cparams("arbitrary", "arbitrary", "arbitrary"),
        name="fox_prompt_attn",
    )(q, kb, vb, bias)


def _fox_sample_kernel(pt_ref, q_ref, kn_ref, vn_ref, bn_ref, e_ref, tri_ref, kpool, vpool, lpool,
                       o_ref, kbuf, vbuf, lbuf, sem, *, nkv, hd, n_chunks, rows_q, group):
    b = pl.program_id(0)
    nb = pl.num_programs(0)
    page_rows = PAGE_SIZE * nkv
    lc = PAGES_PER_CHUNK * PAGE_SIZE
    nh = lbuf.shape[2]
    pages_per_block = SCAN_BLOCK // PAGE_SIZE
    nblk = lc // SCAN_BLOCK

    def start_chunk(bb, step, slot):
        c = n_chunks - 1 - step
        for p in range(PAGES_PER_CHUNK):
            page = pt_ref[bb, c * PAGES_PER_CHUNK + p]
            src = pl.ds(pl.multiple_of(page * page_rows, page_rows), page_rows)
            dst = pl.ds(p * page_rows, page_rows)
            pltpu.make_async_copy(kpool.at[src, :], kbuf.at[slot, dst, :], sem.at[0, slot]).start()
            pltpu.make_async_copy(vpool.at[src, :], vbuf.at[slot, dst, :], sem.at[1, slot]).start()
            pltpu.make_async_copy(lpool.at[page], lbuf.at[slot, p], sem.at[2, slot]).start()

    def wait_chunk(slot):
        for buf, s in ((kbuf, 0), (vbuf, 1), (lbuf, 2)):
            pltpu.make_async_copy(buf.at[slot], buf.at[slot], sem.at[s, slot]).wait()

    ahead = DECODE_SLOTS - 1

    @pl.when(b == 0)
    def _():
        for g in range(ahead):
            start_chunk(g // n_chunks, g % n_chunks, g % DECODE_SLOTS)

    e = e_ref[...]
    tri = tri_ref[...]
    q = [q_ref[0, h] for h in range(nkv)]

    def chunk(step, carry):
        g = b * n_chunks + step
        slot = lax.rem(g, DECODE_SLOTS)
        nxt = g + ahead

        @pl.when(nxt < nb * n_chunks)
        def _():
            start_chunk(lax.div(nxt, n_chunks), lax.rem(nxt, n_chunks), lax.rem(nxt, DECODE_SLOTS))

        wait_chunk(slot)

        xs = [jnp.concatenate([lbuf[slot, k * pages_per_block + p] for p in range(pages_per_block)], axis=1)
              for k in range(nblk)]
        r = _dot(jnp.concatenate([part for x in xs for part in _split3(x)], axis=0), tri)
        sfx_carry = carry[nkv]
        ys = [None] * nblk
        for k in range(nblk - 1, -1, -1):
            local = r[(3 * k) * nh:(3 * k + 1) * nh] + r[(3 * k + 1) * nh:(3 * k + 2) * nh] \
                + r[(3 * k + 2) * nh:(3 * k + 3) * nh]
            ys[k] = local + sfx_carry
            sfx_carry = sfx_carry + (local[:, 0:1] + xs[k][:, 0:1])
        bias = _dot(e, jnp.concatenate(_split3(jnp.concatenate(ys, axis=1) * LOG2E), axis=0))
        new = []
        for h in range(nkv):
            m_old, l_old, acc_old = carry[h]
            k = kbuf[slot, pl.ds(h, lc, stride=nkv), :].astype(BF16)
            v = vbuf[slot, pl.ds(h, lc, stride=nkv), :].astype(BF16)
            s = _dot_nt(q[h], k) + bias[h * rows_q:(h + 1) * rows_q]
            m_new = jnp.maximum(m_old, jnp.max(s, axis=-1, keepdims=True))
            p = jnp.exp2(s - m_new)
            alpha = jnp.exp2(m_old - m_new)
            new.append((m_new, alpha * l_old + jnp.sum(p, axis=-1, keepdims=True),
                        alpha * acc_old + _dot(p.astype(BF16), v)))
        return tuple(new) + (sfx_carry,)

    init = tuple((jnp.full((rows_q, 1), NEG, F32), jnp.zeros((rows_q, 1), F32),
                  jnp.zeros((rows_q, hd), F32)) for _ in range(nkv)) + (jnp.zeros((nh, 1), F32),)
    state = lax.fori_loop(0, n_chunks, chunk, init)

    tn = kn_ref.shape[1]
    bias_n = _dot_nt(e, bn_ref[0])
    r = lax.broadcasted_iota(jnp.int32, (rows_q, tn), 0)
    c = lax.broadcasted_iota(jnp.int32, (rows_q, tn), 1)
    ok = c * group <= r
    for h in range(nkv):
        m_old, l_old, acc_old = state[h]
        k = kn_ref[0, :, h * hd:(h + 1) * hd].astype(BF16)
        v = vn_ref[0, :, h * hd:(h + 1) * hd].astype(BF16)
        s = jnp.where(ok, _dot_nt(q[h], k) + bias_n[h * rows_q:(h + 1) * rows_q], NEG)
        m_new = jnp.maximum(m_old, jnp.max(s, axis=-1, keepdims=True))
        p = jnp.exp2(s - m_new)
        alpha = jnp.exp2(m_old - m_new)
        l_new = alpha * l_old + jnp.sum(p, axis=-1, keepdims=True)
        acc = alpha * acc_old + _dot(p.astype(BF16), v)
        o_ref[0, h] = (acc / l_new).astype(o_ref.dtype)


def _fox_sample_attn(page_table, q, k_new, v_new, bias_new3, emat, kpool, vpool, lpool, nkv, hd, group):
    nb, _, rows_q, _ = q.shape
    n_pages = page_table.shape[1]
    nh = lpool.shape[1]
    n_chunks = n_pages // PAGES_PER_CHUNK
    buf_rows = PAGES_PER_CHUNK * PAGE_SIZE * nkv
    tn = k_new.shape[1]
    i = lax.broadcasted_iota(jnp.int32, (SCAN_BLOCK, SCAN_BLOCK), 0)
    j = lax.broadcasted_iota(jnp.int32, (SCAN_BLOCK, SCAN_BLOCK), 1)
    tri = jnp.where(i > j, 1.0, 0.0).astype(BF16)
    const = lambda a: pl.BlockSpec(a.shape, lambda b, pt: (0,) * a.ndim)
    grid_spec = pltpu.PrefetchScalarGridSpec(
        num_scalar_prefetch=1,
        grid=(nb,),
        in_specs=[pl.BlockSpec((1, nkv, rows_q, hd), lambda b, pt: (b, 0, 0, 0)),
                  pl.BlockSpec((1, tn, nkv * hd), lambda b, pt: (b, 0, 0)),
                  pl.BlockSpec((1, tn, nkv * hd), lambda b, pt: (b, 0, 0)),
                  pl.BlockSpec((1,) + bias_new3.shape[1:], lambda b, pt: (b, 0, 0)),
                  const(emat), const(tri),
                  pl.BlockSpec(memory_space=pl.ANY),
                  pl.BlockSpec(memory_space=pl.ANY),
                  pl.BlockSpec(memory_space=pl.ANY)],
        out_specs=pl.BlockSpec((1, nkv, rows_q, hd), lambda b, pt: (b, 0, 0, 0)),
        scratch_shapes=[pltpu.VMEM((DECODE_SLOTS, buf_rows, hd), F32), pltpu.VMEM((DECODE_SLOTS, buf_rows, hd), F32),
                        pltpu.VMEM((DECODE_SLOTS, PAGES_PER_CHUNK, nh, PAGE_SIZE), F32),
                        pltpu.SemaphoreType.DMA((3, DECODE_SLOTS))],
    )
    return pl.pallas_call(
        functools.partial(_fox_sample_kernel, nkv=nkv, hd=hd, n_chunks=n_chunks, rows_q=rows_q, group=group),
        grid_spec=grid_spec,
        out_shape=jax.ShapeDtypeStruct(q.shape, BF16),
        compiler_params=_cparams("arbitrary"),
        name="fox_sample_attn",
    )(page_table, q, k_new, v_new, bias_new3, emat, tri, kpool, vpool, lpool)


def _out_proj_kernel(o_ref, h_ref, w_ref, g_ref, h1_ref, c_ref):
    o = o_ref[...]
    tn = 512
    for c in range(h_ref.shape[1] // tn):
        cs = slice(c * tn, (c + 1) * tn)
        h1_ref[:, cs] = h_ref[:, cs] + _dot(o, w_ref[:, cs].astype(BF16))
    c_ref[...] = (_rms(h1_ref[...]) * g_ref[...]).astype(BF16)


def _out_proj(o, h, w, g):
    m, d = h.shape
    tm = min(ROW_TILE, m)
    row = lambda n: pl.BlockSpec((tm, n), lambda i: (i, 0))
    return pl.pallas_call(
        _out_proj_kernel,
        grid=(m // tm,),
        in_specs=[row(o.shape[1]), row(d), _weight_spec(w), _const_spec(g.shape)],
        out_specs=[row(d), row(d)],
        out_shape=[jax.ShapeDtypeStruct((m, d), F32), jax.ShapeDtypeStruct((m, d), BF16)],
        compiler_params=_cparams("arbitrary"),
        name="attn_out_proj",
    )(o, h, w, g)


def _ffn_begin(j, acc_ref):
    @pl.when(j == 0)
    def _():
        acc_ref[...] = jnp.zeros(acc_ref.shape, F32)


def _ffn_epilogue(j, h_ref, gn_ref, acc_ref, h2_ref, n_ref):
    @pl.when(j == pl.num_programs(1) - 1)
    def _():
        h2 = h_ref[...] + acc_ref[...]
        h2_ref[...] = h2
        n_ref[...] = (_rms(h2) * gn_ref[...]).astype(BF16)


def _ffn_prompt_kernel(c_ref, wg_ref, wv_ref, cwg_ref, cwv_ref, cbg_ref, cbv_ref, wo_ref, h_ref, gn_ref,
                       h2_ref, n_ref, tg_ref, tv_ref, acc_ref, ug_ref, uv_ref, pg_ref, pv_ref, *, tiles_per_seq):
    i, j = pl.program_id(0), pl.program_id(1)
    tm = c_ref.shape[0]
    first = lax.rem(i, tiles_per_seq) == 0
    _ffn_begin(j, acc_ref)

    @pl.when((i == 0) & (j == 0))
    def _():
        pg_ref[...] = jnp.zeros(pg_ref.shape, F32)
        pv_ref[...] = jnp.zeros(pv_ref.shape, F32)

    x = c_ref[...]
    for u_ref, p_ref, w_ref, t_ref in ((ug_ref, pg_ref, wg_ref, tg_ref), (uv_ref, pv_ref, wv_ref, tv_ref)):
        u_ref[0:HALO, :] = jnp.where(first, 0.0, p_ref[j])
        u_ref[HALO:, :] = _dot(x, w_ref[...])
        tail = u_ref[tm:tm + HALO, :]
        p_ref[j] = tail
        t_ref[...] = tail

    def conv(u_ref, cw_ref, cb_ref):
        y = cb_ref[...]
        for k in range(CONV_W):
            y = y + cw_ref[k:k + 1, :] * u_ref[pl.ds(HALO - (CONV_W - 1) + k, tm), :]
        return y

    act = (jax.nn.gelu(conv(ug_ref, cwg_ref, cbg_ref), approximate=True) * conv(uv_ref, cwv_ref, cbv_ref)).astype(BF16)
    acc_ref[...] += _dot(act, wo_ref[...].astype(BF16))
    _ffn_epilogue(j, h_ref, gn_ref, acc_ref, h2_ref, n_ref)


def _ffn_sample_kernel(c_ref, pg_ref, pv_ref, wg_ref, wv_ref, cwg_ref, cwv_ref, cbg_ref, cbv_ref, wo_ref,
                       h_ref, gn_ref, h2_ref, n_ref, tg_ref, tv_ref, acc_ref):
    j = pl.program_id(1)
    nb = pg_ref.shape[1]
    nt = c_ref.shape[0] // nb
    _ffn_begin(j, acc_ref)
    x = c_ref[...]

    def conv(u, p_ref, cw_ref, cb_ref, t_ref):
        slabs = [p_ref[k] for k in range(CONV_W - 1)] + [u[t * nb:(t + 1) * nb] for t in range(nt)]
        for k in range(CONV_W - 1):
            t_ref[k] = slabs[nt + k]
        y = cb_ref[...]
        for k in range(CONV_W):
            y = y + cw_ref[k:k + 1, :] * jnp.concatenate(slabs[k:k + nt], axis=0)
        return y

    g = conv(_dot(x, wg_ref[...]), pg_ref, cwg_ref, cbg_ref, tg_ref)
    v = conv(_dot(x, wv_ref[...]), pv_ref, cwv_ref, cbv_ref, tv_ref)
    act = (jax.nn.gelu(g, approximate=True) * v).astype(BF16)
    acc_ref[...] += _dot(act, wo_ref[...].astype(BF16))
    _ffn_epilogue(j, h_ref, gn_ref, acc_ref, h2_ref, n_ref)


def _conv_ffn(c, h, layer, w_in, conv_w, conv_b, w_out, g_next, past, tiles_per_seq):
    m, d = h.shape
    dff = w_out.shape[1]
    tf = FFN_TILE
    nf = dff // tf
    tm = min(FFN_ROWS, m)
    wspecs = [pl.BlockSpec((None, d, tf), lambda i, j: (layer, 0, j)),
              pl.BlockSpec((None, d, tf), lambda i, j: (layer, 0, nf + j)),
              pl.BlockSpec((None, CONV_W, tf), lambda i, j: (layer, 0, j)),
              pl.BlockSpec((None, CONV_W, tf), lambda i, j: (layer, 0, nf + j)),
              pl.BlockSpec((None, 1, tf), lambda i, j: (layer, 0, j)),
              pl.BlockSpec((None, 1, tf), lambda i, j: (layer, 0, nf + j)),
              pl.BlockSpec((None, tf, d), lambda i, j: (layer, j, 0))]
    wargs = [w_in, w_in, conv_w, conv_w, conv_b, conv_b, w_out]
    row = pl.BlockSpec((tm, d), lambda i, j: (i, 0))
    tail_specs = [row, pl.BlockSpec((1, d), lambda i, j: (0, 0))]
    if past is None:
        head_specs = [row]
        head_args = [c]
        kern = functools.partial(_ffn_prompt_kernel, tiles_per_seq=tiles_per_seq)
        scratch = [pltpu.VMEM((tm + HALO, tf), F32)] * 2 + [pltpu.VMEM((nf, HALO, tf), F32)] * 2
        state = jax.ShapeDtypeStruct((m // tm, HALO, dff), F32)
        state_spec = pl.BlockSpec((None, HALO, tf), lambda i, j: (i, 0, j))
    else:
        scratch = []
        nb = past.shape[2]
        head_specs = [row, pl.BlockSpec((None, CONV_W - 1, nb, tf), lambda i, j: (layer, 0, 0, j)),
                      pl.BlockSpec((None, CONV_W - 1, nb, tf), lambda i, j: (layer, 0, 0, nf + j))]
        head_args = [c, past, past]
        kern = _ffn_sample_kernel
        state = jax.ShapeDtypeStruct((CONV_W - 1, nb, dff), F32)
        state_spec = pl.BlockSpec((CONV_W - 1, nb, tf), lambda i, j: (0, 0, j))
    return pl.pallas_call(
        kern,
        grid=(m // tm, nf),
        in_specs=head_specs + wspecs + tail_specs,
        out_specs=[row, row, state_spec, state_spec],
        out_shape=[jax.ShapeDtypeStruct((m, d), F32), jax.ShapeDtypeStruct((m, d), BF16), state, state],
        scratch_shapes=[pltpu.VMEM((tm, d), F32)] + scratch,
        compiler_params=_cparams("arbitrary", "arbitrary"),
        name="conv_ffn_prompt" if past is None else "conv_ffn_sample",
    )(*head_args, *wargs, h, g_next)


def _ple_kernel(h_ref, n_ref, p_ref, wp_ref, wg_ref, *rest, n_norms):
    g_refs = rest[:n_norms]
    h3_ref = rest[n_norms]
    a_refs = rest[n_norms + 1:]
    n = n_ref[...]
    p = p_ref[...].astype(BF16)
    d = h_ref.shape[1]
    tn = 512
    for c in range(d // tn):
        cs = slice(c * tn, (c + 1) * tn)
        gate = jax.nn.sigmoid(_dot(n, wg_ref[:, cs].astype(BF16)))
        h3_ref[:, cs] = h_ref[:, cs] + _dot(p, wp_ref[:, cs].astype(BF16)) * gate
    if n_norms:
        xhat = _rms(h3_ref[...])
        for g_ref, a_ref in zip(g_refs, a_refs):
            a_ref[...] = (xhat * g_ref[...]).astype(BF16)


def _ple(h, n, p, layer, wp, wg, gains):
    m, d = h.shape
    tm = min(ROW_TILE, m)
    row = lambda w: pl.BlockSpec((tm, w), lambda i: (i, 0))
    k = len(gains)
    outs = pl.pallas_call(
        functools.partial(_ple_kernel, n_norms=k),
        grid=(m // tm,),
        in_specs=[row(d), row(d), pl.BlockSpec((None, tm, p.shape[2]), lambda i: (layer, i, 0)),
                  _weight_spec(wp, layer), _weight_spec(wg, layer)]
                 + [_const_spec(g.shape) for g in gains],
        out_specs=[row(d)] * (k + 1),
        out_shape=[jax.ShapeDtypeStruct((m, d), F32)] + [jax.ShapeDtypeStruct((m, d), BF16)] * k,
        compiler_params=_cparams("arbitrary"),
        name="ple",
    )(h, n, p, wp, wg, *gains)
    return outs


def _swa_proj_kernel(a_ref, akv_ref, wq_ref, wkv_ref, qg_ref, kg_ref, q_ref, k_ref, v_ref, *, nkv, hd):
    a = a_ref[...]
    qg = qg_ref[...]
    lane = lax.broadcasted_iota(jnp.int32, (a.shape[0], LANES), 1)
    low = lane < hd
    for c in range(wq_ref.shape[1] // MXU_N):
        z = _dot(a, wq_ref[:, c * MXU_N:(c + 1) * MXU_N].astype(BF16))
        for s in range(MXU_N // LANES):
            zz = z[:, s * LANES:(s + 1) * LANES]
            sq = zz * zz
            s_lo = jnp.sum(jnp.where(low, sq, 0.0), axis=-1, keepdims=True)
            s_hi = jnp.sum(jnp.where(low, 0.0, sq), axis=-1, keepdims=True)
            inv = jnp.where(low, lax.rsqrt(s_lo / hd + EPS), lax.rsqrt(s_hi / hd + EPS))
            lo = c * MXU_N + s * LANES
            q_ref[:, lo:lo + LANES] = (zz * inv * qg).astype(BF16)
    akv = akv_ref[...]
    kg = kg_ref[...]
    for c in range(nkv // MXU_N):
        z = _dot(akv, wkv_ref[:, c * MXU_N:(c + 1) * MXU_N])
        for s in range(MXU_N // LANES):
            lo = c * MXU_N + s * LANES
            k_ref[:, lo:lo + LANES] = _rms(z[:, s * LANES:(s + 1) * LANES]) * kg
    for c in range(nkv // MXU_N):
        v_ref[:, c * MXU_N:(c + 1) * MXU_N] = _dot(akv, wkv_ref[:, nkv + c * MXU_N:nkv + (c + 1) * MXU_N])


def _swa_proj(a, akv, wq, wkv_dup, qg, kg, hd):
    m, d = a.shape
    nkv = wkv_dup.shape[2] // 2
    nq = wq.shape[2]
    tm = min(ROW_TILE, m)
    row = lambda n: pl.BlockSpec((tm, n), lambda i: (i, 0))
    return pl.pallas_call(
        functools.partial(_swa_proj_kernel, nkv=nkv, hd=hd),
        grid=(m // tm,),
        in_specs=[row(d), row(d), _weight_spec(wq), _weight_spec(wkv_dup),
                  _const_spec(qg.shape), _const_spec(kg.shape)],
        out_specs=[row(nq), row(nkv), row(nkv)],
        out_shape=[jax.ShapeDtypeStruct((m, nq), BF16), jax.ShapeDtypeStruct((m, nkv), F32),
                   jax.ShapeDtypeStruct((m, nkv), F32)],
        compiler_params=_cparams("arbitrary"),
        name="swa_proj",
    )(a, akv, wq, wkv_dup, qg, kg)


def _swa_prompt_kernel(slope_ref, sink_ref, q_ref, kp_ref, kc_ref, vp_ref, vc_ref, o_ref, *, nkv, hd):
    i = pl.program_id(1)
    w = q_ref.shape[0]
    pairs = q_ref.shape[1] // (nkv * LANES)
    lane = lax.broadcasted_iota(jnp.int32, (w, LANES), 1)
    low = lane < hd
    r = lax.broadcasted_iota(jnp.int32, (w, 2 * w), 0)
    c = lax.broadcasted_iota(jnp.int32, (w, 2 * w), 1)
    dist = r + w - c
    valid = (dist >= 0) & (dist < w) & ((c >= w) | (i > 0))
    base = jnp.where(valid, dist.astype(F32), MASK_DIST)
    top = lax.broadcasted_iota(jnp.int32, (2 * w, 1), 0) < w
    for h in range(nkv):
        k = jnp.concatenate([kp_ref[:, h * LANES:(h + 1) * LANES], kc_ref[:, h * LANES:(h + 1) * LANES]],
                            axis=0).astype(BF16)
        v = jnp.concatenate([vp_ref[:, h * LANES:(h + 1) * LANES], vc_ref[:, h * LANES:(h + 1) * LANES]],
                            axis=0).astype(BF16)
        for pr in range(pairs):
            col = (h * pairs + pr) * LANES
            h0 = 2 * (h * pairs + pr)
            qp = q_ref[:, col:col + LANES]
            rows = jnp.concatenate([jnp.where(low, qp, jnp.zeros_like(qp)),
                                    jnp.where(low, jnp.zeros_like(qp), qp)], axis=0)
            s = _dot_nt(rows, k) - jnp.concatenate([slope_ref[h0] * base, slope_ref[h0 + 1] * base], axis=0)
            sink = jnp.where(top, sink_ref[h0], sink_ref[h0 + 1])
            m = jnp.maximum(jnp.max(s, axis=-1, keepdims=True), sink)
            p = jnp.exp2(s - m)
            den = jnp.sum(p, axis=-1, keepdims=True) + jnp.exp2(sink - m)
            o = _dot(p.astype(BF16), v) / den
            o_ref[:, col:col + LANES] = jnp.where(low, o[:w], o[w:]).astype(o_ref.dtype)


def _swa_prompt_attn(q, kdup, vdup, slopes, sinks, batch, seq, nkv, hd):
    nblk = seq // WINDOW
    dq = q.shape[1]
    dk = kdup.shape[1]
    cur = lambda b, i: (b * nblk + i, 0)
    prev = lambda b, i: (b * nblk + jnp.maximum(i - 1, 0), 0)
    smem = pl.BlockSpec(memory_space=pltpu.SMEM)
    return pl.pallas_call(
        functools.partial(_swa_prompt_kernel, nkv=nkv, hd=hd),
        grid=(batch, nblk),
        in_specs=[smem, smem, pl.BlockSpec((WINDOW, dq), cur), pl.BlockSpec((WINDOW, dk), prev),
                  pl.BlockSpec((WINDOW, dk), cur), pl.BlockSpec((WINDOW, dk), prev),
                  pl.BlockSpec((WINDOW, dk), cur)],
        out_specs=pl.BlockSpec((WINDOW, dq), cur),
        out_shape=jax.ShapeDtypeStruct(q.shape, BF16),
        compiler_params=_cparams("arbitrary", "arbitrary"),
        name="swa_prompt_attn",
    )(slopes, sinks, q, kdup, kdup, vdup, vdup)


def _swa_sample_kernel(q_ref, kc_ref, vc_ref, kn_ref, vn_ref, slope_ref, sink_ref, place_ref, o_ref, wk_ref, wv_ref,
                       *, heads, t_new):
    rows = q_ref.shape[1]
    w = kc_ref.shape[2]
    tn = kn_ref.shape[1]
    slope = slope_ref[...]
    sink = sink_ref[...]
    place = place_ref[...]
    is_new = lax.broadcasted_iota(jnp.int32, kc_ref.shape[1:], 1) >= w - t_new
    t_p = lax.broadcasted_iota(jnp.int32, (rows, w), 0) // heads
    j_p = lax.broadcasted_iota(jnp.int32, (rows, w), 1)
    dist_p = t_p + w - j_p
    bias_p = jnp.where(dist_p < WINDOW, slope * dist_p.astype(F32), -NEG)
    t_n = lax.broadcasted_iota(jnp.int32, (rows, tn), 0) // heads
    j_n = lax.broadcasted_iota(jnp.int32, (rows, tn), 1)
    dist_n = t_n - j_n
    bias_n = jnp.where(dist_n >= 0, slope * dist_n.astype(F32), -NEG)
    for b in range(q_ref.shape[0]):
        q = q_ref[b]
        s_p = _dot(q, kc_ref[b].astype(BF16)) - bias_p
        s_n = _dot_nt(q, kn_ref[b].astype(BF16)) - bias_n
        m = jnp.maximum(jnp.maximum(jnp.max(s_p, axis=-1, keepdims=True), jnp.max(s_n, axis=-1, keepdims=True)),
                        sink)
        p_p = jnp.exp2(s_p - m)
        p_n = jnp.exp2(s_n - m)
        den = jnp.sum(p_p, axis=-1, keepdims=True) + jnp.sum(p_n, axis=-1, keepdims=True) + jnp.exp2(sink - m)
        o = _dot_nt(p_p.astype(BF16), vc_ref[b].astype(BF16)) + _dot(p_n.astype(BF16), vn_ref[b].astype(BF16))
        o_ref[b] = o / den
        for c_ref, n_ref, w_ref in ((kc_ref, kn_ref, wk_ref), (vc_ref, vn_ref, wv_ref)):
            new_t = sum(_dot(place, part) for part in _split3(n_ref[b]))
            w_ref[b] = jnp.where(is_new, new_t.T, pltpu.roll(c_ref[b], w - t_new, axis=1))


def _swa_sample_attn(q, kct, vct, kn, vn, slope_col, sink_col, heads, t_new):
    nb, rows, dk = q.shape
    w = kct.shape[2]
    g = SWA_SAMPLE_ROWS
    place = (jnp.arange(w)[:, None] == w - t_new + jnp.arange(kn.shape[1])[None, :]).astype(BF16)
    blk = lambda a: pl.BlockSpec((g,) + a.shape[1:], lambda b: (b, 0, 0))
    return pl.pallas_call(
        functools.partial(_swa_sample_kernel, heads=heads, t_new=t_new),
        grid=(nb // g,),
        in_specs=[blk(q), blk(kct), blk(vct), blk(kn), blk(vn), _const_spec(slope_col.shape),
                  _const_spec(sink_col.shape), _const_spec(place.shape)],
        out_specs=[blk(q), blk(kct), blk(vct)],
        out_shape=[jax.ShapeDtypeStruct((nb, rows, dk), F32), jax.ShapeDtypeStruct(kct.shape, F32),
                   jax.ShapeDtypeStruct(vct.shape, F32)],
        compiler_params=_cparams("arbitrary"),
        name="swa_sample_attn",
    )(q, kct, vct, kn, vn, slope_col, sink_col, place)


def kernel(x_prompt, x_sample, cache_fox_k, cache_fox_v, cache_fox_logf, cache_win_k, cache_win_v, state_conv, page_table, p_prompt, p_sample, norm_attn_g, norm_ffn_g, norm_ple_g, fox_w_in, fox_b_f, fox_q_norm_g, fox_k_norm_g, fox_w_out, kv_norm_g, swa_w_kv, swa_k_norm_g, swa_w_q, swa_q_norm_g, swa_sinks, swa_w_out, ffn_w_in, ffn_conv_w, ffn_conv_b, ffn_w_out, ple_w_proj, ple_w_gate):
    B, S, D = x_prompt.shape
    NB, T, _ = x_sample.shape
    HD_A = fox_q_norm_g.shape[-1]
    H_A = fox_b_f.shape[-1]
    KVH_A = cache_fox_k.shape[3]
    G_A = H_A // KVH_A
    HD_B = swa_q_norm_g.shape[-1]
    H_B = swa_sinks.shape[-1]
    KVH_B = cache_win_k.shape[2]
    G_B = H_B // KVH_B
    WIN = cache_win_k.shape[1]
    DFF = ffn_w_out.shape[1]
    NQ_A, NK_A = H_A * HD_A, KVH_A * HD_A
    row2 = lambda g: g.reshape(1, -1)

    bf_a = fox_b_f
    qg_a = row2(fox_q_norm_g[0] * (HD_A ** -0.5 * LOG2E))
    kg_a = row2(fox_k_norm_g[0])
    dup = lambda w: jnp.repeat(w.reshape(D, KVH_B, 1, HD_B), LANES // HD_B, axis=2).reshape(D, KVH_B * LANES)
    wk_b, wv_b = jnp.split(swa_w_kv, 2, axis=-1)
    wkv_b = jnp.concatenate([dup(wk_b), dup(wv_b)], axis=1).astype(BF16)[None]
    qg_b = row2(jnp.tile(swa_q_norm_g[0] * (HD_B ** -0.5 * LOG2E), LANES // HD_B))
    kg_b = row2(jnp.tile(swa_k_norm_g, LANES // HD_B))
    ffn_wi = ffn_w_in.astype(BF16)
    ffn_wo = ffn_w_out
    slopes = jnp.exp2(-8.0 * jnp.arange(1, H_B + 1, dtype=F32) / H_B) * LOG2E
    sinks = swa_sinks[0] * LOG2E

    xp = x_prompt.reshape(B * S, D)
    xs = x_sample.transpose(1, 0, 2).reshape(T * NB, D)
    pp = p_prompt.reshape(2, B * S, -1)
    ps = p_sample.transpose(0, 2, 1, 3).reshape(2, T * NB, -1)
    past = state_conv.transpose(0, 2, 1, 3)

    g0 = row2(norm_attn_g[0])
    w_in_t = fox_w_in.transpose(0, 2, 1)
    q_p, k_p, v_p, kb_p, vb_p, lf_p = _fox_proj(xp, g0, w_in_t, bf_a, qg_a, kg_a, NQ_A, NK_A)
    q_s, k_s, v_s, _, _, lf_s = _fox_proj(xs, g0, w_in_t, bf_a, qg_a, kg_a, NQ_A, NK_A)

    bias_p = _scan_lanes(lf_p.reshape(B, S, H_A).transpose(0, 2, 1).reshape(B * H_A, S), suffix=False)
    o_p = _fox_prompt_attn(q_p, kb_p, vb_p, bias_p.reshape(B, KVH_A, G_A, S), B, S, KVH_A, HD_A)

    tn = 16
    lf_new = lf_s.reshape(T, NB, H_A).transpose(1, 2, 0).reshape(NB * H_A, T)
    bias_new = _scan_lanes(jnp.pad(lf_new, ((0, 0), (0, SCAN_BLOCK - T))), suffix=False)[:, :tn]
    bias_new3 = jnp.concatenate(_split3(bias_new.reshape(NB, H_A, tn).transpose(0, 2, 1) * LOG2E), axis=-1)
    rows_q = T * G_A
    head_of_row = (jnp.arange(KVH_A)[:, None] * G_A + jnp.arange(rows_q)[None, :] % G_A).reshape(-1)
    emat = jnp.tile(jax.nn.one_hot(head_of_row, H_A, dtype=BF16), (1, 3))
    qs_b = q_s.reshape(T, NB, KVH_A, G_A, HD_A).transpose(1, 2, 0, 3, 4).reshape(NB, KVH_A, rows_q, HD_A)
    pad_new = lambda a: jnp.pad(a.reshape(T, NB, NK_A).transpose(1, 0, 2), ((0, 0), (0, tn - T), (0, 0)))
    kpool = cache_fox_k.reshape(-1, HD_A)
    vpool = cache_fox_v.reshape(-1, HD_A)
    o_s = _fox_sample_attn(page_table, qs_b, pad_new(k_s), pad_new(v_s), bias_new3, emat, kpool, vpool,
                           cache_fox_logf.reshape(cache_fox_logf.shape[1:]).transpose(0, 2, 1), KVH_A, HD_A, G_A)
    o_s = o_s.reshape(NB, KVH_A, T, G_A, HD_A).transpose(2, 0, 1, 3, 4).reshape(T * NB, NQ_A)

    gf0 = row2(norm_ffn_g[0])
    h_p, c_p = _out_proj(o_p, xp, fox_w_out, gf0)
    h_s, c_s = _out_proj(o_s, xs, fox_w_out, gf0)

    gp0 = row2(norm_ple_g[0])
    tiles_per_seq = S // FFN_ROWS
    conv_b = ffn_conv_b[:, None, :]
    h_p, n_p, *state_p0 = _conv_ffn(c_p, h_p, 0, ffn_wi, ffn_conv_w, conv_b, ffn_wo, gp0, None, tiles_per_seq)
    h_s, n_s, *state_s0 = _conv_ffn(c_s, h_s, 0, ffn_wi, ffn_conv_w, conv_b, ffn_wo, gp0, past, 1)

    gq1, gkv = row2(norm_attn_g[1]), row2(kv_norm_g)
    h_p, a_p, akv_p = _ple(h_p, n_p, pp, 0, ple_w_proj, ple_w_gate, [gq1, gkv])
    h_s, a_s, akv_s = _ple(h_s, n_s, ps, 0, ple_w_proj, ple_w_gate, [gq1, gkv])

    q1_p, kd_p, vd_p = _swa_proj(a_p, akv_p, swa_w_q, wkv_b, qg_b, kg_b, HD_B)
    q1_s, kd_s, vd_s = _swa_proj(a_s, akv_s, swa_w_q, wkv_b, qg_b, kg_b, HD_B)
    undup = lambda a: a.reshape(a.shape[0], KVH_B, LANES // HD_B, HD_B)[:, :, 0]
    o1_p = _swa_prompt_attn(q1_p, kd_p, vd_p, slopes, sinks, B, S, KVH_B, HD_B)

    ksh_s = undup(kd_s).reshape(T, NB, KVH_B * HD_B).transpose(1, 0, 2)
    vsh_s = undup(vd_s).reshape(T, NB, KVH_B * HD_B).transpose(1, 0, 2)
    q1h = q1_s.reshape(T, NB, H_B, 1, HD_B).transpose(1, 0, 2, 3, 4)
    slab = (jnp.arange(H_B)[:, None] // G_B == jnp.arange(KVH_B)[None, :])[None, None, :, :, None]
    q1x = jnp.where(slab, q1h, jnp.zeros((), BF16)).reshape(NB, T * H_B, KVH_B * HD_B)
    padn = lambda a: jnp.pad(a, ((0, 0), (0, tn - T), (0, 0)))
    slope_col = jnp.tile(slopes, T).reshape(T * H_B, 1)
    sink_col = jnp.tile(sinks, T).reshape(T * H_B, 1)
    feature_major = lambda a: a.transpose(0, 2, 3, 1).reshape(NB, KVH_B * HD_B, WIN)
    o1x, wk_s, wv_s = _swa_sample_attn(q1x, feature_major(cache_win_k), feature_major(cache_win_v),
                                       padn(ksh_s), padn(vsh_s), slope_col, sink_col, H_B, T)
    o1x = o1x.reshape(NB, T, H_B, KVH_B, HD_B)
    o1_s = jnp.sum(jnp.where(slab, o1x, 0.0), axis=3)
    o1_s = o1_s.transpose(1, 0, 2, 3).reshape(T * NB, H_B * HD_B).astype(BF16)

    gf1 = row2(norm_ffn_g[1])
    h_p, c_p = _out_proj(o1_p, h_p, swa_w_out, gf1)
    h_s, c_s = _out_proj(o1_s, h_s, swa_w_out, gf1)

    gp1 = row2(norm_ple_g[1])
    h_p, n_p, *state_p1 = _conv_ffn(c_p, h_p, 1, ffn_wi, ffn_conv_w, conv_b, ffn_wo, gp1, None, tiles_per_seq)
    h_s, n_s, *state_s1 = _conv_ffn(c_s, h_s, 1, ffn_wi, ffn_conv_w, conv_b, ffn_wo, gp1, past, 1)

    (y_p,) = _ple(h_p, n_p, pp, 1, ple_w_proj, ple_w_gate, [])
    (y_s,) = _ple(h_s, n_s, ps, 1, ple_w_proj, ple_w_gate, [])

    to_bt = lambda a, *tail: a.reshape((T, NB) + tail).swapaxes(0, 1)
    y_prompt = y_p.reshape(B, S, D)
    y_sample = to_bt(y_s, D)
    fk_p = k_p.reshape(1, B, S, KVH_A, HD_A)
    fv_p = v_p.reshape(1, B, S, KVH_A, HD_A)
    flf_p = lf_p.reshape(1, B, S, H_A)
    last_win = lambda a: undup(a.reshape(B, S, -1)[:, S - WIN:].reshape(B * WIN, -1)).reshape(B, WIN, KVH_B, HD_B)
    win_k_p = last_win(kd_p)
    win_v_p = last_win(vd_p)
    conv_p = jnp.stack([jnp.concatenate([t[tiles_per_seq - 1::tiles_per_seq, HALO - (CONV_W - 1):] for t in st],
                                        axis=-1) for st in (state_p0, state_p1)])
    conv_s = jnp.stack([jnp.concatenate(st, axis=-1).swapaxes(0, 1) for st in (state_s0, state_s1)])
    fk_s = to_bt(k_s, KVH_A, HD_A)[None]
    fv_s = to_bt(v_s, KVH_A, HD_A)[None]
    flf_s = to_bt(lf_s, H_A)[None]
    position_major = lambda a: a.reshape(NB, KVH_B, HD_B, WIN).transpose(0, 3, 1, 2)
    win_k_s = position_major(wk_s)
    win_v_s = position_major(wv_s)
    return (y_prompt, y_sample, fk_p, fv_p, flf_p, win_k_p, win_v_p, conv_p,
            fk_s, fv_s, flf_s, win_k_s, win_v_s, conv_s)
```
